```python
import jax, jax.numpy as jnp
from jax import lax
import numpy as np

D_MODEL = 1024
BATCH = 2
SEQ = 8192
DEPTH = 4
DEC_BATCH = 128
DEC_SEQ = 1
PAST_LEN = 2048
PAGE_SIZE = 128

N_MIXERS = 2
N_MLSTM = (DEPTH + 1) // 2
N_DSWA = DEPTH // 2
EPS = 1e-6

M_HEADS = 4
M_INNER = 2 * D_MODEL
M_DV = M_INNER // M_HEADS
M_DQK = M_DV // 2
M_CHUNK = 64
M_QK = M_HEADS * M_DQK
M_SPLITS = (M_QK, 2 * M_QK, 2 * M_QK + M_INNER, 2 * M_QK + 2 * M_INNER,
            2 * M_QK + 3 * M_INNER, 2 * M_QK + 3 * M_INNER + M_HEADS)
M_PROJ = 2 * M_QK + 3 * M_INNER + 2 * M_HEADS

A_GROUPS = ((128, 1), (512, 4), (2048, 16))
N_GROUPS = 3
A_HEADS = 8
A_DH = 128
A_GW = A_HEADS * A_DH
A_BLK = 128
A_QKV = 3 * N_GROUPS * A_GW
A_PROJ = A_QKV + A_GW
ROPE_THETA = 10000.0

kernel_name = "mlstm_dilated_swa_hybrid_step"


def rms_norm(x, g):
    xf = x.astype(jnp.float32)
    y = xf * lax.rsqrt(jnp.mean(xf * xf, -1, keepdims=True) + EPS)
    return (y * g.astype(jnp.float32)).astype(x.dtype)


def rope(x, pos):
    half = A_DH // 2
    inv = ROPE_THETA ** (-jnp.arange(half, dtype=jnp.float32) / half)
    ang = pos.astype(jnp.float32)[:, None] * inv[None, :]
    shp = (ang.shape[0],) + (1,) * (x.ndim - 3) + (half,)
    cos = jnp.cos(ang).reshape(shp)
    sin = jnp.sin(ang).reshape(shp)
    x1, x2 = x[..., :half], x[..., half:]
    return jnp.concatenate([x1 * cos - x2 * sin, x2 * cos + x1 * sin], axis=-1)


def mlstm_chunk(state, chunk):
    C, n, m = state
    q, k, v, ig, lf = chunk
    L = q.shape[2]
    b = jnp.cumsum(lf, axis=-1)
    g = b[..., -1]
    causal = jnp.tril(jnp.ones((L, L), dtype=bool))
    dlog = b[..., :, None] - b[..., None, :] + ig[..., None, :]
    dlog = jnp.where(causal, dlog, -jnp.inf)
    inter = b + m[..., None]
    m_t = jnp.maximum(inter, jnp.max(dlog, axis=-1))
    w_inter = jnp.exp(inter - m_t)
    w_intra = jnp.exp(dlog - m_t[..., None]) * jnp.einsum('bhtd,bhsd->bhts', q, k)
    num = (w_inter[..., None] * jnp.einsum('bhtd,bhde->bhte', q, C)
           + jnp.einsum('bhts,bhse->bhte', w_intra, v))
    den = w_inter * jnp.einsum('bhtd,bhd->bht', q, n) + jnp.sum(w_intra, axis=-1)
    h = num / jnp.maximum(jnp.abs(den), jnp.exp(-m_t))[..., None]
    a = ig + g[..., None] - b
    m_new = jnp.maximum(g + m, jnp.max(a, axis=-1))
    decay = jnp.exp(g + m - m_new)
    wa = jnp.exp(a - m_new[..., None])
    C_new = decay[..., None, None] * C + jnp.einsum('bhs,bhsd,bhse->bhde', wa, k, v)
    n_new = decay[..., None] * n + jnp.einsum('bhs,bhsd->bhd', wa, k)
    return (C_new, n_new, m_new), h


def mlstm_mixer(xn, state, w_in, b_i, b_f, g_h, w_out, chunk):
    B, S, _ = xn.shape
    f32 = jnp.float32
    p = jnp.einsum('bsd,dp->bsp', xn, w_in)
    q, k, v, o, z, ig, fg = jnp.split(p, M_SPLITS, axis=-1)

    def heads(t, dh):
        return t.reshape(B, S, M_HEADS, dh).transpose(0, 2, 1, 3).astype(f32)

    qh = heads(q, M_DQK)
    kh = heads(k, M_DQK) * (M_DQK ** -0.5)
    vh = heads(v, M_DV)
    igh = (ig + b_i).astype(f32).transpose(0, 2, 1)
    lfh = jax.nn.log_sigmoid((fg + b_f).astype(f32)).transpose(0, 2, 1)
    nc = S // chunk

    def to_chunks(t):
        t = t.reshape(t.shape[:2] + (nc, chunk) + t.shape[3:])
        return jnp.moveaxis(t, 2, 0)

    state, h = lax.scan(mlstm_chunk, state, tuple(to_chunks(t) for t in (qh, kh, vh, igh, lfh)))
    h = jnp.moveaxis(h, 0, 2).reshape(B, M_HEADS, S, M_DV).transpose(0, 2, 1, 3)
    h = h * lax.rsqrt(jnp.mean(h * h, axis=-1, keepdims=True) + EPS)
    h = h.reshape(B, S, M_INNER) * g_h.astype(f32)
    y = h * jax.nn.sigmoid(o.astype(f32)) * jax.nn.silu(z.astype(f32))
    return jnp.einsum('bse,ed->bsd', y.astype(xn.dtype), w_out), state


def attn_project(xn, w_in, g_q, g_k, pos):
    B, S, _ = xn.shape
    p = jnp.einsum('bsd,dp->bsp', xn, w_in)
    qkv = p[..., :A_QKV].reshape(B, S, 3, N_GROUPS, A_HEADS, A_DH).astype(jnp.float32)
    z = p[..., A_QKV:]

    def head_norm(t, g):
        t = t * lax.rsqrt(jnp.mean(t * t, axis=-1, keepdims=True) + EPS)
        return t * g.astype(jnp.float32)[:, None, :]

    q = rope(head_norm(qkv[:, :, 0], g_q), pos)
    k = rope(head_norm(qkv[:, :, 1], g_k), pos)
    v = qkv[:, :, 2]
    return q, k, v, z


def dilated_prompt(q, k, v, dil, span):
    B, S, h, dh = q.shape
    Ls = S // dil
    nb = -(-Ls // A_BLK)
    Lp = nb * A_BLK

    def sub(t):
        t = t.reshape(B, Ls, dil, h, dh).transpose(0, 2, 1, 3, 4)
        t = jnp.pad(t, ((0, 0), (0, 0), (0, Lp - Ls), (0, 0), (0, 0)))
        return t.reshape(B, dil, nb, A_BLK, h, dh)

    def with_prev(t):
        prev = jnp.pad(t, ((0, 0), (0, 0), (1, 0), (0, 0), (0, 0), (0, 0)))[:, :, :-1]
        return jnp.concatenate([prev, t], axis=3)

    qs = sub(q)
    kk = with_prev(sub(k))
    vv = with_prev(sub(v))
    s = jnp.einsum('brnqhd,brnkhd->brnhqk', qs, kk) * (dh ** -0.5)
    iq = jnp.arange(A_BLK)[:, None]
    ik = jnp.arange(2 * A_BLK)[None, :]
    diff = A_BLK + iq - ik
    mk = (jnp.arange(nb)[:, None, None] - 1) * A_BLK + ik
    mask = (diff >= 0) & (diff <= span) & (mk >= 0)
    s = jnp.where(mask[:, None], s, -jnp.inf)
    mx = jnp.max(s, axis=-1, keepdims=True)
    pr = jnp.exp(s - mx)
    den = jnp.sum(pr, axis=-1, keepdims=True)
    o = jnp.einsum('brnhqk,brnkhd->brnqhd', pr / den, vv)
    lse = (mx + jnp.log(den))[..., 0]
    o = o.reshape(B, dil, Lp, h, dh)[:, :, :Ls].transpose(0, 2, 1, 3, 4).reshape(B, S, h, dh)
    lse = lse.transpose(0, 1, 2, 4, 3).reshape(B, dil, Lp, h)[:, :, :Ls]
    lse = lse.transpose(0, 2, 1, 3).reshape(B, S, h)
    return o, lse


def dilated_sample(q, k, v, kbuf, vbuf, dil, span):
    Lb = kbuf.shape[1]
    DS = q.shape[1]
    dh = q.shape[-1]
    idx = Lb + jnp.arange(DS)[:, None] - dil * jnp.arange(span + 1)[None, :]
    valid = idx >= 0
    from_buf = (idx < Lb)[None, :, :, None, None]
    bi = jnp.clip(idx, 0, Lb - 1)
    ni = jnp.clip(idx - Lb, 0, DS - 1)
    kg = jnp.where(from_buf, kbuf[:, bi].astype(jnp.float32), k[:, ni])
    vg = jnp.where(from_buf, vbuf[:, bi].astype(jnp.float32), v[:, ni])
    s = jnp.einsum('bshd,bsjhd->bshj', q, kg) * (dh ** -0.5)
    s = jnp.where(valid[None, :, None, :], s, -jnp.inf)
    mx = jnp.max(s, axis=-1, keepdims=True)
    pr = jnp.exp(s - mx)
    den = jnp.sum(pr, axis=-1, keepdims=True)
    o = jnp.einsum('bshj,bsjhd->bshd', pr / den, vg)
    lse = (mx + jnp.log(den))[..., 0]
    return o, lse


def dswa_mixer(xn, pos, w_in, g_q, g_k, w_out, bufs):
    B, S, _ = xn.shape
    q, k, v, z = attn_project(xn, w_in, g_q, g_k, pos)
    outs, lses, new_rows = [], [], []
    for g, (win, dil) in enumerate(A_GROUPS):
        span = win // dil
        if bufs is None:
            o, l = dilated_prompt(q[:, :, g], k[:, :, g], v[:, :, g], dil, span)
            keep = min(win, S)
            new_rows += [k[:, S - keep:, g], v[:, S - keep:, g]]
        else:
            o, l = dilated_sample(q[:, :, g], k[:, :, g], v[:, :, g], bufs[2 * g], bufs[2 * g + 1], dil, span)
            new_rows += [k[:, :, g], v[:, :, g]]
        outs.append(o)
        lses.append(l)
    w = jax.nn.softmax(jnp.stack(lses, 0), axis=0)
    y = jnp.einsum('gbsh,gbshd->bshd', w, jnp.stack(outs, 0)).reshape(B, S, A_GW)
    y = y * jax.nn.silu(z.astype(jnp.float32))
    out = jnp.einsum('bse,ed->bsd', y.astype(xn.dtype), w_out)
    return out, [r.astype(xn.dtype) for r in new_rows]


def setup_inputs(seed: int = 0) -> dict:
    key = jax.random.key(seed)
    ks = jax.random.split(key, 24)
    f32 = jnp.float32

    def nrm(k, shape, s=1.0):
        return s * jax.random.normal(k, shape, f32)

    lb = [min(w, PAST_LEN) for w, _ in A_GROUPS]
    cshape = lambda L: (N_DSWA, DEC_BATCH, L, A_HEADS, A_DH)
    return {
        "x_prompt": nrm(ks[0], (BATCH, SEQ, D_MODEL)),
        "x_sample": nrm(ks[1], (DEC_BATCH, DEC_SEQ, D_MODEL)),
        "state_C": nrm(ks[2], (N_MLSTM, DEC_BATCH, M_HEADS, M_DQK, M_DV), M_DQK ** -0.5),
        "state_n": nrm(ks[3], (N_MLSTM, DEC_BATCH, M_HEADS, M_DQK), M_DQK ** -0.5),
        "state_m": nrm(ks[4], (N_MLSTM, DEC_BATCH, M_HEADS)),
        "cache_k1": nrm(ks[5], cshape(lb[0])),
        "cache_v1": nrm(ks[6], cshape(lb[0])),
        "cache_k2": nrm(ks[7], cshape(lb[1])),
        "cache_v2": nrm(ks[8], cshape(lb[1])),
        "cache_k3": nrm(ks[9], cshape(lb[2])),
        "cache_v3": nrm(ks[10], cshape(lb[2])),
        "ln_g": 1.0 + nrm(ks[11], (DEPTH, D_MODEL), 0.02),
        "m_w_in": nrm(ks[12], (N_MLSTM, D_MODEL, M_PROJ), D_MODEL ** -0.5),
        "m_b_i": nrm(ks[13], (N_MLSTM, M_HEADS), 0.1),
        "m_b_f": jnp.linspace(3.0, 6.0, M_HEADS, dtype=f32)[None, :] + nrm(ks[14], (N_MLSTM, M_HEADS), 0.1),
        "m_g_h": 1.0 + nrm(ks[15], (N_MLSTM, M_INNER), 0.02),
        "m_w_out": nrm(ks[16], (N_MLSTM, M_INNER, D_MODEL), M_INNER ** -0.5),
        "a_w_in": nrm(ks[17], (N_DSWA, D_MODEL, A_PROJ), D_MODEL ** -0.5),
        "a_g_q": 1.0 + nrm(ks[18], (N_DSWA, N_GROUPS, A_DH), 0.02),
        "a_g_k": 1.0 + nrm(ks[19], (N_DSWA, N_GROUPS, A_DH), 0.02),
        "a_w_out": nrm(ks[20], (N_DSWA, A_GW, D_MODEL), A_GW ** -0.5),
    }


def reference(x_prompt, x_sample, state_C, state_n, state_m, cache_k1, cache_v1, cache_k2, cache_v2,
              cache_k3, cache_v3, ln_g, m_w_in, m_b_i, m_b_f, m_g_h, m_w_out, a_w_in, a_g_q, a_g_k, a_w_out):
    f32 = jnp.float32
    B, S, _ = x_prompt.shape
    DB, DS, _ = x_sample.shape
    pos_p = jnp.arange(S)
    pos_s = PAST_LEN + jnp.arange(DS)
    yp, ys = x_prompt, x_sample
    caches = (cache_k1, cache_v1, cache_k2, cache_v2, cache_k3, cache_v3)
    mp_states, ms_states = [], []
    ap_rows, as_rows = [], []
    for i in range(DEPTH):
        j = i // N_MIXERS
        if i % N_MIXERS == 0:
            w = (m_w_in[j], m_b_i[j], m_b_f[j], m_g_h[j], m_w_out[j])
            init = (jnp.zeros((B, M_HEADS, M_DQK, M_DV), f32), jnp.zeros((B, M_HEADS, M_DQK), f32),
                    jnp.zeros((B, M_HEADS), f32))
            out, st_p = mlstm_mixer(rms_norm(yp, ln_g[i]), init, *w, chunk=min(M_CHUNK, S))
            yp = yp + out
            past = (state_C[j].astype(f32), state_n[j].astype(f32), state_m[j].astype(f32))
            out, st_s = mlstm_mixer(rms_norm(ys, ln_g[i]), past, *w, chunk=DS)
            ys = ys + out
            mp_states.append(st_p)
            ms_states.append(st_s)
        else:
            w = (a_w_in[j], a_g_q[j], a_g_k[j], a_w_out[j])
            out, rows_p = dswa_mixer(rms_norm(yp, ln_g[i]), pos_p, *w, None)
            yp = yp + out
            out, rows_s = dswa_mixer(rms_norm(ys, ln_g[i]), pos_s, *w, [c[j] for c in caches])
            ys = ys + out
            ap_rows.append(rows_p)
            as_rows.append(rows_s)
    dt = x_prompt.dtype
    C_p = jnp.stack([s[0] for s in mp_states]).astype(dt)
    n_p = jnp.stack([s[1] for s in mp_states]).astype(dt)
    m_p = jnp.stack([s[2] for s in mp_states]).astype(dt)
    C_s = jnp.stack([s[0] for s in ms_states]).astype(dt)
    n_s = jnp.stack([s[1] for s in ms_states]).astype(dt)
    m_s = jnp.stack([s[2] for s in ms_states]).astype(dt)
    k1_p = jnp.stack([r[0] for r in ap_rows])
    v1_p = jnp.stack([r[1] for r in ap_rows])
    k2_p = jnp.stack([r[2] for r in ap_rows])
    v2_p = jnp.stack([r[3] for r in ap_rows])
    k3_p = jnp.stack([r[4] for r in ap_rows])
    v3_p = jnp.stack([r[5] for r in ap_rows])
    k1_s = jnp.stack([r[0] for r in as_rows])
    v1_s = jnp.stack([r[1] for r in as_rows])
    k2_s = jnp.stack([r[2] for r in as_rows])
    v2_s = jnp.stack([r[3] for r in as_rows])
    k3_s = jnp.stack([r[4] for r in as_rows])
    v3_s = jnp.stack([r[5] for r in as_rows])
    return (yp, ys, C_p, n_p, m_p, k1_p, v1_p, k2_p, v2_p, k3_p, v3_p,
            C_s, n_s, m_s, k1_s, v1_s, k2_s, v2_s, k3_s, v3_s)
```

```python
import functools

import jax
import jax.numpy as jnp
from jax import lax
from jax.experimental import pallas as pl
from jax.experimental.pallas import tpu as pltpu

F32 = jnp.float32
BF16 = jnp.bfloat16
EPS = 1e-6
LANES = 128
VMEM_LIMIT = 56 * 1024 * 1024

A_GROUPS = ((128, 1), (512, 4), (2048, 16))
PAST_LEN = 2048
ROPE_THETA = 10000.0
A_BLK = 128
M_BLOCK = 256


def _params(n_axes):
    return pltpu.CompilerParams(dimension_semantics=("arbitrary",) * n_axes, vmem_limit_bytes=VMEM_LIMIT)


def _sigmoid(x):
    return 1.0 / (1.0 + jnp.exp(-x))


def _log_sigmoid(x):
    return jnp.minimum(x, 0.0) - jnp.log1p(jnp.exp(-jnp.abs(x)))


def _rms_rows(x, g):
    ms = jnp.mean(x * x, axis=-1, keepdims=True)
    return x * lax.rsqrt(ms + EPS) * g


def _split3(x):
    x1 = x.astype(BF16)
    r1 = x - x1.astype(F32)
    x2 = r1.astype(BF16)
    x3 = (r1 - x2.astype(F32)).astype(BF16)
    return x1, x2, x3


def _dot(a, b):
    return jnp.dot(a, b, preferred_element_type=F32)


def _dot_nt(a, b):
    return lax.dot_general(a, b, (((1,), (1,)), ((), ())), preferred_element_type=F32)


def _dot_tn(a, b):
    return lax.dot_general(a, b, (((0,), (0,)), ((), ())), preferred_element_type=F32)


def _proj_mlstm_kernel(x_ref, g_ref, w_ref, wg_ref, o_ref, gate_ref, xn_ref):
    @pl.when(pl.program_id(1) == 0)
    def _():
        xn = _rms_rows(x_ref[...], g_ref[...]).astype(BF16)
        xn_ref[...] = xn
        gate_ref[...] = _dot(xn, wg_ref[...])

    o_ref[...] = _dot(xn_ref[...], w_ref[...]).astype(o_ref.dtype)


def _proj_mlstm(x, g, w, wg, *, tm, tn, out_dtype):
    T, D = x.shape
    P = w.shape[1]
    return pl.pallas_call(
        _proj_mlstm_kernel,
        out_shape=(jax.ShapeDtypeStruct((T, P), out_dtype), jax.ShapeDtypeStruct((T, LANES), F32)),
        grid=(T // tm, P // tn),
        in_specs=[
            pl.BlockSpec((tm, D), lambda i, j: (i, 0)),
            pl.BlockSpec((1, D), lambda i, j: (0, 0)),
            pl.BlockSpec((D, tn), lambda i, j: (0, j)),
            pl.BlockSpec((D, LANES), lambda i, j: (0, 0)),
        ],
        out_specs=(
            pl.BlockSpec((tm, tn), lambda i, j: (i, j)),
            pl.BlockSpec((tm, LANES), lambda i, j: (i, 0)),
        ),
        scratch_shapes=[pltpu.VMEM((tm, D), BF16)],
        compiler_params=_params(2),
        name="proj_mlstm",
    )(x, g, w, wg)


def _proj_attn_kernel(x_ref, g_ref, w_ref, gain_ref, cos_ref, sin_ref, o_ref, xn_ref, *, n_rot, dh, sub):
    j = pl.program_id(1)

    @pl.when(j == 0)
    def _():
        xn_ref[...] = _rms_rows(x_ref[...], g_ref[...]).astype(BF16)

    tn = o_ref.shape[1]

    @pl.when(j < n_rot)
    def _():
        cos = cos_ref[...]
        sin = sin_ref[...]
        for s in range(tn // sub):
            acc = _dot(xn_ref[...], w_ref[:, s * sub:(s + 1) * sub])
            for t in range(sub // dh):
                c0 = s * sub + t * dh
                a = _rms_rows(acc[:, t * dh:(t + 1) * dh], gain_ref[:, c0:c0 + dh])
                a = a * cos + pltpu.roll(a, dh // 2, axis=1) * sin
                o_ref[:, c0:c0 + dh] = a.astype(o_ref.dtype)

    @pl.when(j >= n_rot)
    def _():
        o_ref[...] = _dot(xn_ref[...], w_ref[...]).astype(o_ref.dtype)


def _proj_attn(x, g, w, gain, cos, sin, *, tm, tn, n_rot, dh, out_dtype):
    T, D = x.shape
    P = w.shape[1]
    pos_blocks = cos.shape[0] // tm
    kern = functools.partial(_proj_attn_kernel, n_rot=n_rot, dh=dh, sub=2 * dh)
    return pl.pallas_call(
        kern,
        out_shape=jax.ShapeDtypeStruct((T, P), out_dtype),
        grid=(T // tm, P // tn),
        in_specs=[
            pl.BlockSpec((tm, D), lambda i, j: (i, 0)),
            pl.BlockSpec((1, D), lambda i, j: (0, 0)),
            pl.BlockSpec((D, tn), lambda i, j: (0, j)),
            pl.BlockSpec((1, tn), lambda i, j: (0, j)),
            pl.BlockSpec((tm, dh), lambda i, j: (i % pos_blocks, 0)),
            pl.BlockSpec((tm, dh), lambda i, j: (i % pos_blocks, 0)),
        ],
        out_specs=pl.BlockSpec((tm, tn), lambda i, j: (i, j)),
        scratch_shapes=[pltpu.VMEM((tm, D), BF16)],
        compiler_params=_params(2),
        name="proj_attn",
    )(x, g, w, gain, cos, sin)


def _outproj_kernel(y_ref, w_ref, r_ref, o_ref):
    o_ref[...] = r_ref[...] + _dot(y_ref[...], w_ref[...])


def _outproj(y, w, resid, *, tm):
    T, E = y.shape
    D = w.shape[1]
    return pl.pallas_call(
        _outproj_kernel,
        out_shape=jax.ShapeDtypeStruct((T, D), F32),
        grid=(T // tm,),
        in_specs=[
            pl.BlockSpec((tm, E), lambda i: (i, 0)),
            pl.BlockSpec((E, D), lambda i: (0, 0)),
            pl.BlockSpec((tm, D), lambda i: (i, 0)),
        ],
        out_specs=pl.BlockSpec((tm, D), lambda i: (i, 0)),
        compiler_params=_params(1),
        name="outproj",
    )(y, w, resid)


def _head_of(col, dh):
    return lax.shift_right_logical(col, dh.bit_length() - 1)


def _head_expander(nh, dh):
    r = lax.broadcasted_iota(jnp.int32, (LANES, nh * dh), 0)
    c = lax.broadcasted_iota(jnp.int32, (LANES, nh * dh), 1)
    return jnp.where(_head_of(c, dh) == r, 1.0, 0.0).astype(BF16)


def _outproj_comb_kernel(o1_ref, o2_ref, o3_ref, l1_ref, l2_ref, l3_ref, z_ref, w_ref, r_ref, out_ref, *, nh, dh):
    l1, l2, l3 = l1_ref[...], l2_ref[...], l3_ref[...]
    mx = jnp.maximum(jnp.maximum(l1, l2), l3)
    e1, e2, e3 = jnp.exp(l1 - mx), jnp.exp(l2 - mx), jnp.exp(l3 - mx)
    inv = 1.0 / (e1 + e2 + e3)
    expand = _head_expander(nh, dh)
    y = None
    for e, o_ref in ((e1, o1_ref), (e2, o2_ref), (e3, o3_ref)):
        w_hi, w_mid, w_lo = _split3(e * inv)
        wide = _dot(w_hi, expand) + _dot(w_mid, expand) + _dot(w_lo, expand)
        term = wide * o_ref[...].astype(F32)
        y = term if y is None else y + term
    z = z_ref[...].astype(F32)
    y = y * (z * _sigmoid(z))
    out_ref[...] = r_ref[...] + _dot(y.astype(BF16), w_ref[...])


def _outproj_comb(os_, ls_, p, w, resid, *, tm, nh, dh, z_block):
    T, E = os_[0].shape
    D = w.shape[1]
    kern = functools.partial(_outproj_comb_kernel, nh=nh, dh=dh)
    o_spec = pl.BlockSpec((tm, E), lambda i: (i, 0))
    l_spec = pl.BlockSpec((tm, LANES), lambda i: (i, 0))
    return pl.pallas_call(
        kern,
        out_shape=jax.ShapeDtypeStruct((T, D), F32),
        grid=(T // tm,),
        in_specs=[o_spec, o_spec, o_spec, l_spec, l_spec, l_spec,
                  pl.BlockSpec((tm, E), lambda i: (i, z_block)),
                  pl.BlockSpec((E, D), lambda i: (0, 0)),
                  pl.BlockSpec((tm, D), lambda i: (i, 0))],
        out_specs=pl.BlockSpec((tm, D), lambda i: (i, 0)),
        compiler_params=_params(1),
        name="outproj_comb",
    )(*os_, *ls_, p, w, resid)


def _mlstm_prompt_kernel(q_ref, k_ref, v_ref, o_ref, z_ref, gt_ref, bias_ref, gh_ref,
                         y_ref, c_ref, n_ref, m_ref, *, H, DQK, DV):
    L = q_ref.shape[0]
    scale = DQK ** -0.5

    @pl.when(pl.program_id(1) == 0)
    def _():
        c_ref[...] = jnp.zeros_like(c_ref)
        n_ref[...] = jnp.zeros_like(n_ref)
        m_ref[...] = jnp.zeros_like(m_ref)

    gates = gt_ref[...] + bias_ref[...]
    lane = lax.broadcasted_iota(jnp.int32, gates.shape, 1)
    x = jnp.where(lane < H, gates, _log_sigmoid(gates))
    row = lax.broadcasted_iota(jnp.int32, (L, L), 0)
    col = lax.broadcasted_iota(jnp.int32, (L, L), 1)
    causal = col <= row
    tri = jnp.where(causal, 1.0, 0.0).astype(BF16)
    x1, x2, x3 = _split3(x)
    cum = _dot(tri, x1) + _dot(tri, x2) + _dot(tri, x3)
    xt = x.T
    cumt = cum.T

    m_all = m_ref[0]
    lane1 = lax.broadcasted_iota(jnp.int32, m_all.shape, 1)
    m_next = m_all
    for h in range(H):
        qs = slice(h * DQK, (h + 1) * DQK)
        vs = slice(h * DV, (h + 1) * DV)
        ig_row = xt[h:h + 1, :]
        ig_col = x[:, h:h + 1]
        b_row = cumt[H + h:H + h + 1, :]
        b_col = cum[:, H + h:H + h + 1]
        g_tot = cumt[H + h:H + h + 1, L - 1:L]
        m_prev = m_all[:, h:h + 1]

        dlog = jnp.where(causal, b_col - b_row + ig_row, -jnp.inf)
        inter = b_col + m_prev
        m_t = jnp.maximum(inter, jnp.max(dlog, axis=1, keepdims=True))
        w_inter = jnp.exp(inter - m_t)
        q = q_ref[:, qs]
        k = k_ref[:, qs]
        v = v_ref[:, vs]
        w_intra = jnp.exp(dlog - m_t) * (_dot_nt(q, k) * scale)
        c_prev = c_ref[0, h]
        num = w_inter * _dot(q, c_prev.astype(BF16)) + _dot(w_intra.astype(BF16), v)
        n_prev = n_ref[0, :, qs]
        den = w_inter * jnp.sum(q.astype(F32) * n_prev, axis=1, keepdims=True) + jnp.sum(w_intra, axis=1, keepdims=True)
        hid = num / jnp.maximum(jnp.abs(den), jnp.exp(-m_t))
        hid = hid * lax.rsqrt(jnp.mean(hid * hid, axis=1, keepdims=True) + EPS)
        og = o_ref[:, vs].astype(F32)
        zg = z_ref[:, vs].astype(F32)
        y_ref[:, vs] = (hid * gh_ref[:, vs] * _sigmoid(og) * (zg * _sigmoid(zg))).astype(BF16)

        a_col = ig_col + g_tot - b_col
        m_new = jnp.maximum(g_tot + m_prev, jnp.max(a_col, axis=0, keepdims=True))
        decay = jnp.exp(g_tot + m_prev - m_new)
        kw = k.astype(F32) * (jnp.exp(a_col - m_new) * scale)
        c_ref[0, h] = decay * c_prev + _dot_tn(kw.astype(BF16), v)
        n_ref[0, :, qs] = decay * n_prev + jnp.sum(kw, axis=0, keepdims=True)
        m_next = jnp.where(lane1 == h, m_new, m_next)
    m_ref[0] = m_next


def _mlstm_prompt(p, gates, bias, gh, *, B, S, H, DQK, DV):
    L = min(M_BLOCK, S)
    nc = S // L
    QK, INNER = H * DQK, H * DV
    kern = functools.partial(_mlstm_prompt_kernel, H=H, DQK=DQK, DV=DV)
    row = lambda b, c: b * nc + c
    v_blk = 2 * QK // INNER
    return pl.pallas_call(
        kern,
        out_shape=(jax.ShapeDtypeStruct((B * S, INNER), BF16),
                   jax.ShapeDtypeStruct((B, H, DQK, DV), F32),
                   jax.ShapeDtypeStruct((B, 1, QK), F32),
                   jax.ShapeDtypeStruct((B, 1, LANES), F32)),
        grid=(B, nc),
        in_specs=[
            pl.BlockSpec((L, QK), lambda b, c: (row(b, c), 0)),
            pl.BlockSpec((L, QK), lambda b, c: (row(b, c), 1)),
            pl.BlockSpec((L, INNER), lambda b, c: (row(b, c), v_blk)),
            pl.BlockSpec((L, INNER), lambda b, c: (row(b, c), v_blk + 1)),
            pl.BlockSpec((L, INNER), lambda b, c: (row(b, c), v_blk + 2)),
            pl.BlockSpec((L, LANES), lambda b, c: (row(b, c), 0)),
            pl.BlockSpec((1, LANES), lambda b, c: (0, 0)),
            pl.BlockSpec((1, INNER), lambda b, c: (0, 0)),
        ],
        out_specs=(
            pl.BlockSpec((L, INNER), lambda b, c: (row(b, c), 0)),
            pl.BlockSpec((1, H, DQK, DV), lambda b, c: (b, 0, 0, 0)),
            pl.BlockSpec((1, 1, QK), lambda b, c: (b, 0, 0)),
            pl.BlockSpec((1, 1, LANES), lambda b, c: (b, 0, 0)),
        ),
        compiler_params=_params(2),
        name="mlstm_prompt",
    )(p, p, p, p, p, gates, bias, gh)


SC_ROWS = 16


def _mlstm_sample_kernel(p_ref, gt_ref, bias_ref, gh_ref, n_ref, m_ref, v_ref, c_ref,
                         y_ref, c_out_ref, n_out_ref, m_out_ref,
                         qkt_ref, sct_ref, hq_ref, a_ref, bc_ref, *, H, DQK, DV, bt):
    step = pl.program_id(0)
    head = pl.program_id(1)
    DB = p_ref.shape[0]
    QK, INNER = H * DQK, H * DV
    scale = DQK ** -0.5

    @pl.when((step == 0) & (head == 0))
    def _():
        gates = gt_ref[...] + bias_ref[...]
        lane = lax.broadcasted_iota(jnp.int32, gates.shape, 1)
        ig = jnp.where(lane < H, gates, 0.0)
        lf = jnp.where(lane < H, pltpu.roll(_log_sigmoid(gates), LANES - H, axis=1), 0.0)
        m_prev = m_ref[...]
        m_t = jnp.maximum(lf + m_prev, ig)
        w_inter = jnp.exp(lf + m_prev - m_t)
        wa = jnp.exp(ig - m_t)
        q = p_ref[:, 0:QK].astype(F32)
        k = p_ref[:, QK:2 * QK].astype(F32)
        n_prev = n_ref[...]
        qk = jnp.zeros_like(gates)
        qn = jnp.zeros_like(gates)
        for h in range(H):
            qs = slice(h * DQK, (h + 1) * DQK)
            qk = jnp.where(lane == h, jnp.sum(q[:, qs] * k[:, qs], axis=1, keepdims=True) * scale, qk)
            qn = jnp.where(lane == h, jnp.sum(q[:, qs] * n_prev[:, qs], axis=1, keepdims=True), qn)
            n_out_ref[:, qs] = w_inter[:, h:h + 1] * n_prev[:, qs] + (wa[:, h:h + 1] * scale) * k[:, qs]
        w_intra = wa * qk
        den = w_inter * qn + w_intra
        inv = 1.0 / jnp.maximum(jnp.abs(den), jnp.exp(-m_t))
        a_ref[...] = w_inter * inv
        bc_ref[...] = w_intra * inv
        m_out_ref[...] = jnp.where(lane < H, m_t, 0.0)
        qkt_ref[...] = p_ref[:, 0:2 * QK].astype(F32).T.astype(BF16)
        sc = jnp.where(lane < H, w_inter, pltpu.roll(wa * scale, H, axis=1))
        sct_ref[...] = jnp.where(lane < 2 * H, sc, 0.0).T[0:SC_ROWS, :]

    r = lax.broadcasted_iota(jnp.int32, (DB, LANES), 0)
    srow = lax.broadcasted_iota(jnp.int32, (SC_ROWS, LANES), 0)
    trow = lax.broadcasted_iota(jnp.int32, (bt, DV), 0)
    qt = qkt_ref[pl.ds(pl.multiple_of(head * DQK, DQK), DQK), :]
    kt = qkt_ref[pl.ds(pl.multiple_of(QK + head * DQK, DQK), DQK), :]
    s1, s2, s3 = _split3(sct_ref[...])
    tile = jnp.zeros((bt, DV), F32)
    for i in range(bt):
        onehot = jnp.where(r == step * bt + i, 1.0, 0.0).astype(BF16)
        qcol = _dot(qt, onehot)
        kcol = _dot(kt, onehot)
        scal = _dot(s1, onehot) + _dot(s2, onehot) + _dot(s3, onehot)
        decay = jnp.sum(jnp.where(srow == head, scal, 0.0), axis=0, keepdims=True)
        wsc = jnp.sum(jnp.where(srow == H + head, scal, 0.0), axis=0, keepdims=True)
        hrow = []
        for t in range(DV // LANES):
            cs = slice(t * LANES, (t + 1) * LANES)
            c_prev = c_ref[i, 0, :, cs]
            hrow.append(jnp.sum(qcol * c_prev, axis=0, keepdims=True))
            c_out_ref[i, 0, :, cs] = decay * c_prev + kcol * (wsc * v_ref[i:i + 1, cs])
        tile = jnp.where(trow == i, jnp.concatenate(hrow, axis=1), tile)
    hq_ref[head, pl.ds(pl.multiple_of(step * bt, bt), bt), :] = tile

    @pl.when((step == pl.num_programs(0) - 1) & (head == H - 1))
    def _():
        for h in range(H):
            vs = slice(h * DV, (h + 1) * DV)
            v = p_ref[:, 2 * QK + h * DV:2 * QK + (h + 1) * DV].astype(F32)
            hid = a_ref[:, h:h + 1] * hq_ref[h] + bc_ref[:, h:h + 1] * v
            hid = hid * lax.rsqrt(jnp.mean(hid * hid, axis=1, keepdims=True) + EPS)
            og = p_ref[:, 2 * QK + INNER + h * DV:2 * QK + INNER + (h + 1) * DV].astype(F32)
            zg = p_ref[:, 2 * QK + 2 * INNER + h * DV:2 * QK + 2 * INNER + (h + 1) * DV].astype(F32)
            y_ref[:, vs] = (hid * gh_ref[:, vs] * _sigmoid(og) * (zg * _sigmoid(zg))).astype(BF16)


def _mlstm_sample(p, gates, bias, gh, c_in, n_in, m_in, *, H, DQK, DV, bt):
    DB = p.shape[0]
    QK, INNER = H * DQK, H * DV
    assert 2 * H <= SC_ROWS and DB % bt == 0
    kern = functools.partial(_mlstm_sample_kernel, H=H, DQK=DQK, DV=DV, bt=bt)
    full = lambda shape: pl.BlockSpec(shape, lambda s, h: (0,) * len(shape))
    c_spec = pl.BlockSpec((bt, 1, DQK, DV), lambda s, h: (s, h, 0, 0))
    v_heads = p[:, 2 * QK:2 * QK + INNER].reshape(DB, H, DV).transpose(1, 0, 2)
    return pl.pallas_call(
        kern,
        out_shape=(jax.ShapeDtypeStruct((DB, INNER), BF16),
                   jax.ShapeDtypeStruct(c_in.shape, F32),
                   jax.ShapeDtypeStruct((DB, QK), F32),
                   jax.ShapeDtypeStruct((DB, LANES), F32)),
        grid=(DB // bt, H),
        in_specs=[full(p.shape), full((DB, LANES)), full((1, LANES)), full((1, INNER)),
                  full((DB, QK)), full((DB, LANES)),
                  pl.BlockSpec((None, bt, DV), lambda s, h: (h, s, 0)), c_spec],
        out_specs=(full((DB, INNER)), c_spec, full((DB, QK)), full((DB, LANES))),
        scratch_shapes=[pltpu.VMEM((2 * QK, DB), BF16), pltpu.VMEM((SC_ROWS, DB), F32),
                        pltpu.VMEM((H, DB, DV), F32), pltpu.VMEM((DB, LANES), F32), pltpu.VMEM((DB, LANES), F32)],
        compiler_params=_params(2),
        name="mlstm_sample",
    )(p, gates, bias, gh, n_in, m_in, v_heads, c_in)


def _attn_prompt_kernel(q_ref, kc_ref, kp_ref, vc_ref, vp_ref, o_ref, lse_ref, *, nh, dh, span):
    n = pl.program_id(2)
    blk = q_ref.shape[0]
    scale = dh ** -0.5
    iq = lax.broadcasted_iota(jnp.int32, (blk, blk), 0)
    ik = lax.broadcasted_iota(jnp.int32, (blk, blk), 1)
    mask_cur = (ik <= iq) & (iq - ik <= span)
    mask_prev = blk + iq - ik <= jnp.where(n > 0, span, -1)
    lane = lax.broadcasted_iota(jnp.int32, (blk, LANES), 1)
    lse_all = jnp.zeros((blk, LANES), F32)
    for h in range(nh):
        hs = slice(h * dh, (h + 1) * dh)
        q = q_ref[:, hs]
        s_cur = jnp.where(mask_cur, _dot_nt(q, kc_ref[:, hs]) * scale, -jnp.inf)
        s_prev = jnp.where(mask_prev, _dot_nt(q, kp_ref[:, hs]) * scale, -jnp.inf)
        mx = jnp.maximum(jnp.max(s_cur, axis=1, keepdims=True), jnp.max(s_prev, axis=1, keepdims=True))
        p_cur = jnp.exp(s_cur - mx)
        p_prev = jnp.exp(s_prev - mx)
        den = jnp.sum(p_cur, axis=1, keepdims=True) + jnp.sum(p_prev, axis=1, keepdims=True)
        inv = 1.0 / den
        o = _dot((p_cur * inv).astype(BF16), vc_ref[:, hs]) + _dot((p_prev * inv).astype(BF16), vp_ref[:, hs])
        o_ref[:, hs] = o.astype(BF16)
        lse_all = jnp.where(lane == h, mx + jnp.log(den), lse_all)
    lse_ref[...] = lse_all


def _attn_prompt(p, *, B, S, g, ng, nh, dh, win, dil):
    GW = nh * dh
    PW = p.shape[1]
    nblk = PW // GW
    Ls = S // dil
    nb = Ls // A_BLK
    p3 = p.reshape(B, Ls, dil * PW)
    kern = functools.partial(_attn_prompt_kernel, nh=nh, dh=dh, span=win // dil)

    def spec(col, prev):
        if prev:
            return pl.BlockSpec((None, A_BLK, GW), lambda b, r, n: (b, jnp.maximum(n - 1, 0), r * nblk + col))
        return pl.BlockSpec((None, A_BLK, GW), lambda b, r, n: (b, n, r * nblk + col))

    o, lse = pl.pallas_call(
        kern,
        out_shape=(jax.ShapeDtypeStruct((B, Ls, dil * GW), BF16),
                   jax.ShapeDtypeStruct((B, Ls, dil * LANES), F32)),
        grid=(B, dil, nb),
        in_specs=[spec(g, False), spec(ng + g, False), spec(ng + g, True),
                  spec(2 * ng + g, False), spec(2 * ng + g, True)],
        out_specs=(pl.BlockSpec((None, A_BLK, GW), lambda b, r, n: (b, n, r)),
                   pl.BlockSpec((None, A_BLK, LANES), lambda b, r, n: (b, n, r))),
        compiler_params=_params(3),
        name=f"attn_prompt_d{dil}",
    )(p3, p3, p3, p3, p3)
    return o.reshape(B * S, GW), lse.reshape(B * S, LANES)


def _attn_sample_kernel(q_ref, kn_ref, vn_ref, kc_ref, vc_ref, o_ref, lse_ref, *, nh, dh):
    bt, GW = q_ref.shape
    scale = dh ** -0.5
    rows = max(nh, 16)
    sub = lax.broadcasted_iota(jnp.int32, (rows, GW), 0)
    ln = lax.broadcasted_iota(jnp.int32, (rows, GW), 1)
    own = _head_of(ln, dh) == sub
    eye = lax.broadcasted_iota(jnp.int32, (rows, LANES), 0) == lax.broadcasted_iota(jnp.int32, (rows, LANES), 1)
    orow_id = lax.broadcasted_iota(jnp.int32, (bt, GW), 0)
    lrow_id = lax.broadcasted_iota(jnp.int32, (bt, LANES), 0)
    o_tile = jnp.zeros((bt, GW), F32)
    l_tile = jnp.zeros((bt, LANES), F32)
    for i in range(bt):
        qbd = jnp.where(own, jnp.broadcast_to(q_ref[i:i + 1, :], (rows, GW)), 0.0)
        s = _dot_nt(qbd.astype(BF16), kc_ref[i].astype(BF16)) * scale
        s0 = jnp.sum(qbd * kn_ref[i:i + 1, :], axis=1, keepdims=True) * scale
        mx = jnp.maximum(jnp.max(s, axis=1, keepdims=True), s0)
        pr = jnp.exp(s - mx)
        p0 = jnp.exp(s0 - mx)
        den = jnp.sum(pr, axis=1, keepdims=True) + p0
        inv = 1.0 / den
        out = _dot((pr * inv).astype(BF16), vc_ref[i].astype(BF16)) + (p0 * inv) * vn_ref[i:i + 1, :]
        orow = jnp.sum(jnp.where(own, out, 0.0), axis=0, keepdims=True)
        lse = jnp.broadcast_to(mx + jnp.log(den), (rows, LANES))
        lrow = jnp.sum(jnp.where(eye, lse, 0.0), axis=0, keepdims=True)
        o_tile = jnp.where(orow_id == i, orow, o_tile)
        l_tile = jnp.where(lrow_id == i, lrow, l_tile)
    o_ref[...] = o_tile
    lse_ref[...] = l_tile


def _attn_sample(p, k_cache, v_cache, *, g, ng, nh, dh, win, dil, bt):
    DB = p.shape[0]
    GW = nh * dh
    span = win // dil
    assert k_cache.shape[1] == win and v_cache.shape[1] == win, "decode path expects a full window of cached rows"
    kv, vv = k_cache.reshape(DB, span, dil * GW), v_cache.reshape(DB, span, dil * GW)
    kern = functools.partial(_attn_sample_kernel, nh=nh, dh=dh)
    row_spec = lambda col: pl.BlockSpec((bt, GW), lambda s: (s, col))
    cache_spec = pl.BlockSpec((bt, span, GW), lambda s: (s, 0, 0))
    return pl.pallas_call(
        kern,
        out_shape=(jax.ShapeDtypeStruct((DB, GW), F32), jax.ShapeDtypeStruct((DB, LANES), F32)),
        grid=(DB // bt,),
        in_specs=[row_spec(g), row_spec(ng + g), row_spec(2 * ng + g), cache_spec, cache_spec],
        out_specs=(pl.BlockSpec((bt, GW), lambda s: (s, 0)), pl.BlockSpec((bt, LANES), lambda s: (s, 0))),
        compiler_params=_params(1),
        name=f"attn_sample_d{dil}",
    )(p, p, p, kv, vv)


def _rope_tables(pos, dh):
    half = dh // 2
    inv = ROPE_THETA ** (-jnp.arange(half, dtype=F32) / half)
    ang = pos.astype(F32)[:, None] * inv[None, :]
    cos, sin = jnp.cos(ang), jnp.sin(ang)
    return jnp.concatenate([cos, cos], axis=1), jnp.concatenate([-sin, sin], axis=1)


def _row_tile(t, cap):
    tm = min(t, cap)
    while t % tm:
        tm //= 2
    return tm


def kernel(x_prompt, x_sample, state_C, state_n, state_m, cache_k1, cache_v1, cache_k2, cache_v2, cache_k3, cache_v3, ln_g, m_w_in, m_b_i, m_b_f, m_g_h, m_w_out, a_w_in, a_g_q, a_g_k, a_w_out):
    B, S, D = x_prompt.shape
    DB, DS, _ = x_sample.shape
    assert DS == 1
    depth = ln_g.shape[0]
    H = m_b_i.shape[1]
    INNER = m_w_out.shape[1]
    QK = (m_w_in.shape[2] - 3 * INNER - 2 * H) // 2
    DQK, DV = QK // H, INNER // H
    ng, dh = a_g_q.shape[1], a_g_q.shape[2]
    GW = a_w_out.shape[1]
    nh = GW // dh
    assert ng == len(A_GROUPS) and dh == LANES
    caches = (cache_k1, cache_v1, cache_k2, cache_v2, cache_k3, cache_v3)

    xp = x_prompt.reshape(B * S, D)
    xs = x_sample.reshape(DB, D)
    tm_p = _row_tile(S, 1024)
    tm_s = DB
    cos_p, sin_p = _rope_tables(jnp.arange(S), dh)
    cos_s, sin_s = _rope_tables(jnp.full((DB,), PAST_LEN), dh)
    ones = jnp.ones((1, (ng + 1) * GW), F32)

    mp, ms, ap, as_ = [], [], [], []
    for i in range(depth):
        j = i // 2
        g = ln_g[i].reshape(1, D)
        if i % 2 == 0:
            w_main = m_w_in[j][:, :2 * QK + 3 * INNER].astype(BF16)
            w_gate = jnp.pad(m_w_in[j][:, 2 * QK + 3 * INNER:], ((0, 0), (0, LANES - 2 * H))).astype(BF16)
            w_out = m_w_out[j].astype(BF16)
            bias = jnp.pad(jnp.concatenate([m_b_i[j], m_b_f[j]]), (0, LANES - 2 * H)).reshape(1, LANES)
            gh = m_g_h[j].reshape(1, INNER)

            p, gates = _proj_mlstm(xp, g, w_main, w_gate, tm=tm_p, tn=1024, out_dtype=BF16)
            y, c_p, n_p, m_p = _mlstm_prompt(p, gates, bias, gh, B=B, S=S, H=H, DQK=DQK, DV=DV)
            xp = _outproj(y, w_out, xp, tm=_row_tile(B * S, 512))
            mp.append((c_p, n_p.reshape(B, H, DQK), m_p[:, 0, :H]))

            p, gates = _proj_mlstm(xs, g, w_main, w_gate, tm=tm_s, tn=1024, out_dtype=F32)
            m_in = jnp.pad(state_m[j], ((0, 0), (0, LANES - H)))
            y, c_s, n_s, m_s = _mlstm_sample(p, gates, bias, gh, state_C[j], state_n[j].reshape(DB, QK), m_in,
                                             H=H, DQK=DQK, DV=DV, bt=8)
            xs = _outproj(y, w_out, xs, tm=tm_s)
            ms.append((c_s, n_s.reshape(DB, H, DQK), m_s[:, :H]))
        else:
            w_in = a_w_in[j].astype(BF16)
            w_out = a_w_out[j].astype(BF16)
            gain = jnp.concatenate([jnp.tile(a_g_q[j], (1, nh)).reshape(1, ng * GW),
                                    jnp.tile(a_g_k[j], (1, nh)).reshape(1, ng * GW), ones], axis=1)

            p = _proj_attn(xp, g, w_in, gain, cos_p, sin_p, tm=tm_p, tn=GW, n_rot=2 * ng, dh=dh, out_dtype=BF16)
            os_, ls_ = [], []
            for gi, (win, dil) in enumerate(A_GROUPS):
                o, lse = _attn_prompt(p, B=B, S=S, g=gi, ng=ng, nh=nh, dh=dh, win=win, dil=dil)
                os_.append(o)
                ls_.append(lse)
            xp = _outproj_comb(os_, ls_, p, w_out, xp, tm=_row_tile(B * S, 512), nh=nh, dh=dh, z_block=3 * ng)
            p4 = p.reshape(B, S, 3 * ng + 1, nh, dh)
            rows = []
            for gi, (win, dil) in enumerate(A_GROUPS):
                keep = min(win, S)
                rows += [p4[:, S - keep:, ng + gi].astype(F32), p4[:, S - keep:, 2 * ng + gi].astype(F32)]
            ap.append(rows)

            p = _proj_attn(xs, g, w_in, gain, cos_s, sin_s, tm=tm_s, tn=GW, n_rot=2 * ng, dh=dh, out_dtype=F32)
            os_, ls_ = [], []
            for gi, (win, dil) in enumerate(A_GROUPS):
                o, lse = _attn_sample(p, caches[2 * gi][j], caches[2 * gi + 1][j], g=gi, ng=ng, nh=nh, dh=dh,
                                      win=win, dil=dil, bt=8)
                os_.append(o)
                ls_.append(lse)
            xs = _outproj_comb(os_, ls_, p, w_out, xs, tm=tm_s, nh=nh, dh=dh, z_block=3 * ng)
            p4 = p.reshape(DB, 1, 3 * ng + 1, nh, dh)
            rows = []
            for gi in range(ng):
                rows += [p4[:, :, ng + gi].astype(F32), p4[:, :, 2 * ng + gi].astype(F32)]
            as_.append(rows)

    stack = lambda items, k: jnp.stack([it[k] for it in items])
    return ((xp.reshape(B, S, D), xs.reshape(DB, DS, D), stack(mp, 0), stack(mp, 1), stack(mp, 2))
            + tuple(stack(ap, k) for k in range(2 * ng))
            + (stack(ms, 0), stack(ms, 1), stack(ms, 2))
            + tuple(stack(as_, k) for k in range(2 * ng)))
```

```python
import functools

import jax
import jax.numpy as jnp
from jax import lax
from jax.experimental import pallas as pl
from jax.experimental.pallas import tpu as pltpu

F32 = jnp.float32
BF16 = jnp.bfloat16
EPS = 1e-6
LANES = 128
VMEM_LIMIT = 56 * 1024 * 1024

A_GROUPS = ((128, 1), (512, 4), (2048, 16))
PAST_LEN = 2048
ROPE_THETA = 10000.0
A_BLK = 128
M_BLOCK = 256
PROJ_UNIT_ROWS = 512


def _params(n_axes):
    return pltpu.CompilerParams(dimension_semantics=("arbitrary",) * n_axes, vmem_limit_bytes=VMEM_LIMIT)


def _sigmoid(x):
    return 1.0 / (1.0 + jnp.exp(-x))


def _log_sigmoid(x):
    return jnp.minimum(x, 0.0) - jnp.log1p(jnp.exp(-jnp.abs(x)))


def _rms_rows(x, g):
    ms = jnp.mean(x * x, axis=-1, keepdims=True)
    return x * lax.rsqrt(ms + EPS) * g


def _split3(x):
    x1 = x.astype(BF16)
    r1 = x - x1.astype(F32)
    x2 = r1.astype(BF16)
    x3 = (r1 - x2.astype(F32)).astype(BF16)
    return x1, x2, x3


def _dot(a, b):
    return jnp.dot(a, b, preferred_element_type=F32)


def _dot_nt(a, b):
    return lax.dot_general(a, b, (((1,), (1,)), ((), ())), preferred_element_type=F32)


def _dot_tn(a, b):
    return lax.dot_general(a, b, (((0,), (0,)), ((), ())), preferred_element_type=F32)


def _proj_mlstm_kernel(x_ref, g_ref, w_ref, wg_ref, o_ref, gate_ref, xn_ref):
    @pl.when(pl.program_id(1) == 0)
    def _():
        xn = _rms_rows(x_ref[...], g_ref[...]).astype(BF16)
        xn_ref[...] = xn
        gate_ref[...] = _dot(xn, wg_ref[...])

    o_ref[...] = _dot(xn_ref[...], w_ref[...]).astype(o_ref.dtype)


def _proj_mlstm(x, g, w, wg, *, tm, tn, out_dtype):
    T, D = x.shape
    P = w.shape[1]
    return pl.pallas_call(
        _proj_mlstm_kernel,
        out_shape=(jax.ShapeDtypeStruct((T, P), out_dtype), jax.ShapeDtypeStruct((T, LANES), F32)),
        grid=(T // tm, P // tn),
        in_specs=[
            pl.BlockSpec((tm, D), lambda i, j: (i, 0)),
            pl.BlockSpec((1, D), lambda i, j: (0, 0)),
            pl.BlockSpec((D, tn), lambda i, j: (0, j)),
            pl.BlockSpec((D, LANES), lambda i, j: (0, 0)),
        ],
        out_specs=(
            pl.BlockSpec((tm, tn), lambda i, j: (i, j)),
            pl.BlockSpec((tm, LANES), lambda i, j: (i, 0)),
        ),
        scratch_shapes=[pltpu.VMEM((tm, D), BF16)],
        compiler_params=_params(2),
        name="proj_mlstm",
    )(x, g, w, wg)


def _proj_attn_kernel(x_ref, g_ref, w_ref, gain_ref, cos_ref, sin_ref, *refs, ng, dh, sub, dils):
    outs, z_ref, xn_ref, acc_ref = refs[:ng], refs[ng], refs[ng + 1], refs[ng + 2]
    j = pl.program_id(1)
    n_slabs, tm, _ = acc_ref.shape

    @pl.when(j == 0)
    def _():
        xn_ref[...] = _rms_rows(x_ref[...], g_ref[...]).astype(BF16)

    @pl.when(j < 3 * ng)
    def _():
        for s in range(n_slabs * dh // sub):
            acc = _dot(xn_ref[...], w_ref[:, s * sub:(s + 1) * sub])
            for t in range(sub // dh):
                acc_ref[s * (sub // dh) + t] = acc[:, t * dh:(t + 1) * dh]

    @pl.when(j == 3 * ng)
    def _():
        z_ref[...] = _dot(xn_ref[...], w_ref[...]).astype(z_ref.dtype)

    def finish(g, dil, rot):
        per_res = tm // dil
        unit_rows = min(tm, PROJ_UNIT_ROWS)
        units = []
        for u in range(tm // unit_rows):
            lo = u * unit_rows
            if per_res >= unit_rows:
                units.append([(lo // per_res, lo % per_res, unit_rows)])
            else:
                units.append([(lo // per_res + k, 0, per_res) for k in range(unit_rows // per_res)])

        def gather(ref, seg, *lead):
            r, i0, n = seg
            rows = pl.ds(i0, n) if dil == 1 else pl.ds(r + dil * i0, n, stride=dil)
            return ref[(*lead, rows, slice(None))]

        if rot:
            kk = lax.broadcasted_iota(jnp.int32, (2 * dh, 2 * dh), 0)
            cc = lax.broadcasted_iota(jnp.int32, (2 * dh, 2 * dh), 1)
            head_sum = jnp.where(_head_of(kk, dh) == _head_of(cc, dh), 1.0, 0.0).astype(BF16)
            half_swap = jnp.where(kk == (cc ^ (dh // 2)), 1.0, 0.0).astype(BF16)
        pairs = range(n_slabs // 2)
        for segs in units:
            vals = [jnp.concatenate(
                [jnp.concatenate([gather(acc_ref, sg, 2 * p + t) for sg in segs], axis=0) for t in range(2)], axis=1)
                for p in pairs]
            if rot:
                cos = jnp.concatenate([gather(cos_ref, sg) for sg in segs], axis=0)
                sin = jnp.concatenate([gather(sin_ref, sg) for sg in segs], axis=0)
                cos, sin = jnp.concatenate([cos, cos], axis=1), jnp.concatenate([sin, sin], axis=1)
                sums = [_dot((a * a).astype(BF16), head_sum) for a in vals]
                vals = [a * lax.rsqrt(ss * (1.0 / dh) + EPS) * gain_ref[:, 2 * p * dh:2 * (p + 1) * dh]
                        for p, (a, ss) in enumerate(zip(vals, sums))]
                his = [a.astype(BF16) for a in vals]
                los = [(a - hi.astype(F32)).astype(BF16) for a, hi in zip(vals, his)]
                swapped = [_dot(hi, half_swap) + _dot(lo, half_swap) for hi, lo in zip(his, los)]
                vals = [a * cos + sw * sin for a, sw in zip(vals, swapped)]
            for p, a in enumerate(vals):
                a = a.astype(outs[g].dtype)
                row = 0
                for r, i0, n in segs:
                    outs[g][r, i0:i0 + n, 2 * p * dh:2 * (p + 1) * dh] = a[row:row + n]
                    row += n

    for g, dil in enumerate(dils):
        @pl.when((j < 2 * ng) & (lax.rem(j, ng) == g))
        def _(g=g, dil=dil):
            finish(g, dil, True)

        @pl.when((j >= 2 * ng) & (j < 3 * ng) & (lax.rem(j, ng) == g))
        def _(g=g, dil=dil):
            finish(g, dil, False)


def _proj_attn(x, g, w, gain, cos, sin, *, B, tm, tn, ng, dh, dils, out_dtype):
    T, D = x.shape
    S = T // B
    tiles = S // tm
    pos_blocks = cos.shape[0] // tm
    assert w.shape[1] == (3 * ng + 1) * tn and all(tm % d == 0 for d in dils)
    kern = functools.partial(_proj_attn_kernel, ng=ng, dh=dh, sub=2 * dh, dils=dils)

    def group_spec(gi, dil):
        return pl.BlockSpec((None, dil, tm // dil, tn),
                            lambda i, j: (i // tiles, 0, i % tiles, jnp.clip(jnp.maximum(j - gi, 0) // ng, 0, 2)))

    *qkvs, z = pl.pallas_call(
        kern,
        out_shape=tuple(jax.ShapeDtypeStruct((B, dil, S // dil, 3 * tn), out_dtype) for dil in dils)
        + (jax.ShapeDtypeStruct((T, tn), out_dtype),),
        grid=(T // tm, 3 * ng + 1),
        in_specs=[
            pl.BlockSpec((tm, D), lambda i, j: (i, 0)),
            pl.BlockSpec((1, D), lambda i, j: (0, 0)),
            pl.BlockSpec((D, tn), lambda i, j: (0, j)),
            pl.BlockSpec((1, tn), lambda i, j: (0, j)),
            pl.BlockSpec((tm, dh), lambda i, j: (i % pos_blocks, 0)),
            pl.BlockSpec((tm, dh), lambda i, j: (i % pos_blocks, 0)),
        ],
        out_specs=tuple(group_spec(gi, dil) for gi, dil in enumerate(dils))
        + (pl.BlockSpec((tm, tn), lambda i, j: (i, 0)),),
        scratch_shapes=[pltpu.VMEM((tm, D), BF16), pltpu.VMEM((tn // dh, tm, dh), F32)],
        compiler_params=_params(2),
        name="proj_attn",
    )(x, g, w, gain, cos, sin)
    return qkvs, z


def _outproj_kernel(y_ref, w_ref, r_ref, o_ref):
    o_ref[...] = r_ref[...] + _dot(y_ref[...].astype(BF16), w_ref[...])


def _outproj(y, w, resid, *, tm):
    T, E = y.shape
    D = w.shape[1]
    return pl.pallas_call(
        _outproj_kernel,
        out_shape=jax.ShapeDtypeStruct((T, D), F32),
        grid=(T // tm,),
        in_specs=[
            pl.BlockSpec((tm, E), lambda i: (i, 0)),
            pl.BlockSpec((E, D), lambda i: (0, 0)),
            pl.BlockSpec((tm, D), lambda i: (i, 0)),
        ],
        out_specs=pl.BlockSpec((tm, D), lambda i: (i, 0)),
        compiler_params=_params(1),
        name="outproj",
    )(y, w, resid)


def _head_of(col, dh):
    return lax.shift_right_logical(col, dh.bit_length() - 1)


def _head_expander(nh, dh):
    r = lax.broadcasted_iota(jnp.int32, (LANES, nh * dh), 0)
    c = lax.broadcasted_iota(jnp.int32, (LANES, nh * dh), 1)
    return jnp.where(_head_of(c, dh) == r, 1.0, 0.0).astype(BF16)


def _outproj_comb_kernel(*refs, ng, nh, dh, dils):
    o_refs, l_refs = refs[:ng], refs[ng:2 * ng]
    z_ref, w_ref, r_ref, out_ref, ot_ref, lt_ref = refs[2 * ng:]
    tm = out_ref.shape[0]
    for g, dil in enumerate(dils):
        for r in range(dil):
            rows = slice(None) if dil == 1 else pl.ds(r, tm // dil, stride=dil)
            lt_ref[g, rows, :] = l_refs[g][r]
            for s in range(nh):
                ot_ref[g, s, rows, :] = o_refs[g][r, :, s * dh:(s + 1) * dh].astype(F32)
    lses = [lt_ref[g] for g in range(ng)]
    mx = functools.reduce(jnp.maximum, lses)
    es = [jnp.exp(l - mx) for l in lses]
    inv = 1.0 / functools.reduce(jnp.add, es)
    expand = _head_expander(nh, dh)
    y = None
    for g in range(ng):
        w_hi, w_lo, _ = _split3(es[g] * inv)
        wide = _dot(w_hi, expand) + _dot(w_lo, expand)
        term = wide * jnp.concatenate([ot_ref[g, s] for s in range(nh)], axis=1)
        y = term if y is None else y + term
    z = z_ref[...].astype(F32)
    y = y * (z * _sigmoid(z))
    out_ref[...] = r_ref[...] + _dot(y.astype(BF16), w_ref[...])


def _outproj_comb(os_, ls_, z, w, resid, *, B, tm, nh, dh, dils):
    T, D = resid.shape
    E = w.shape[0]
    ng = len(dils)
    tiles = T // B // tm
    kern = functools.partial(_outproj_comb_kernel, ng=ng, nh=nh, dh=dh, dils=dils)
    res_spec = lambda dil, width: pl.BlockSpec((None, dil, tm // dil, width), lambda i: (i // tiles, 0, i % tiles, 0))
    return pl.pallas_call(
        kern,
        out_shape=jax.ShapeDtypeStruct((T, D), F32),
        grid=(T // tm,),
        in_specs=[res_spec(dil, E) for dil in dils] + [res_spec(dil, LANES) for dil in dils]
        + [pl.BlockSpec((tm, E), lambda i: (i, 0)),
           pl.BlockSpec((E, D), lambda i: (0, 0)),
           pl.BlockSpec((tm, D), lambda i: (i, 0))],
        out_specs=pl.BlockSpec((tm, D), lambda i: (i, 0)),
        scratch_shapes=[pltpu.VMEM((ng, nh, tm, dh), F32), pltpu.VMEM((ng, tm, LANES), F32)],
        compiler_params=_params(1),
        name="outproj_comb",
    )(*os_, *ls_, z, w, resid)


def _mlstm_prompt_kernel(q_ref, k_ref, v_ref, o_ref, z_ref, gt_ref, bias_ref, gh_ref,
                         y_ref, c_ref, n_ref, m_ref, *, H, DQK, DV):
    L = q_ref.shape[0]
    scale = DQK ** -0.5

    @pl.when(pl.program_id(1) == 0)
    def _():
        c_ref[...] = jnp.zeros_like(c_ref)
        n_ref[...] = jnp.zeros_like(n_ref)
        m_ref[...] = jnp.zeros_like(m_ref)

    gates = gt_ref[...] + bias_ref[...]
    lane = lax.broadcasted_iota(jnp.int32, gates.shape, 1)
    x = jnp.where(lane < H, gates, _log_sigmoid(gates))
    row = lax.broadcasted_iota(jnp.int32, (L, L), 0)
    col = lax.broadcasted_iota(jnp.int32, (L, L), 1)
    causal = col <= row
    tri = jnp.where(causal, 1.0, 0.0).astype(BF16)
    x1, x2, x3 = _split3(x)
    cum = _dot(tri, x1) + _dot(tri, x2) + _dot(tri, x3)
    xt = x.T
    cumt = cum.T

    m_all = m_ref[0]
    lane1 = lax.broadcasted_iota(jnp.int32, m_all.shape, 1)
    m_next = m_all
    qss = [slice(h * DQK, (h + 1) * DQK) for h in range(H)]
    vss = [slice(h * DV, (h + 1) * DV) for h in range(H)]
    qk = [_dot_nt(q_ref[:, qs], k_ref[:, qs]) for qs in qss]
    qc = [_dot(q_ref[:, qs], c_ref[0, h].astype(BF16)) for h, qs in enumerate(qss)]
    mids = []
    for h, qs in enumerate(qss):
        ig_row = xt[h:h + 1, :]
        ig_col = x[:, h:h + 1]
        b_row = cumt[H + h:H + h + 1, :]
        b_col = cum[:, H + h:H + h + 1]
        g_tot = cumt[H + h:H + h + 1, L - 1:L]
        m_prev = m_all[:, h:h + 1]

        dlog = jnp.where(causal, b_col - b_row + ig_row, -jnp.inf)
        inter = b_col + m_prev
        m_t = jnp.maximum(inter, jnp.max(dlog, axis=1, keepdims=True))
        w_inter = jnp.exp(inter - m_t)
        w_intra = jnp.exp(dlog - m_t) * (qk[h] * scale)
        n_prev = n_ref[0, :, qs]
        den = (w_inter * jnp.sum(q_ref[:, qs].astype(F32) * n_prev, axis=1, keepdims=True)
               + jnp.sum(w_intra, axis=1, keepdims=True))
        inv = 1.0 / jnp.maximum(jnp.abs(den), jnp.exp(-m_t))

        a_col = ig_col + g_tot - b_col
        m_new = jnp.maximum(g_tot + m_prev, jnp.max(a_col, axis=0, keepdims=True))
        decay = jnp.exp(g_tot + m_prev - m_new)
        kw = k_ref[:, qs].astype(F32) * (jnp.exp(a_col - m_new) * scale)
        n_ref[0, :, qs] = decay * n_prev + jnp.sum(kw, axis=0, keepdims=True)
        m_next = jnp.where(lane1 == h, m_new, m_next)
        mids.append((w_inter, w_intra.astype(BF16), inv, decay, kw.astype(BF16)))
    m_ref[0] = m_next
    wv = [_dot(mid[1], v_ref[:, vs]) for mid, vs in zip(mids, vss)]
    kv = [_dot_tn(mid[4], v_ref[:, vs]) for mid, vs in zip(mids, vss)]
    for h, vs in enumerate(vss):
        w_inter, _, inv, decay, _ = mids[h]
        hid = (w_inter * qc[h] + wv[h]) * inv
        hid = hid * lax.rsqrt(jnp.mean(hid * hid, axis=1, keepdims=True) + EPS)
        og = o_ref[:, vs].astype(F32)
        zg = z_ref[:, vs].astype(F32)
        y_ref[:, vs] = (hid * gh_ref[:, vs] * _sigmoid(og) * (zg * _sigmoid(zg))).astype(BF16)
        c_ref[0, h] = decay * c_ref[0, h] + kv[h]


def _mlstm_prompt(p, gates, bias, gh, *, B, S, H, DQK, DV):
    L = min(M_BLOCK, S)
    nc = S // L
    QK, INNER = H * DQK, H * DV
    kern = functools.partial(_mlstm_prompt_kernel, H=H, DQK=DQK, DV=DV)
    row = lambda b, c: b * nc + c
    v_blk = 2 * QK // INNER
    return pl.pallas_call(
        kern,
        out_shape=(jax.ShapeDtypeStruct((B * S, INNER), BF16),
                   jax.ShapeDtypeStruct((B, H, DQK, DV), F32),
                   jax.ShapeDtypeStruct((B, 1, QK), F32),
                   jax.ShapeDtypeStruct((B, 1, LANES), F32)),
        grid=(B, nc),
        in_specs=[
            pl.BlockSpec((L, QK), lambda b, c: (row(b, c), 0)),
            pl.BlockSpec((L, QK), lambda b, c: (row(b, c), 1)),
            pl.BlockSpec((L, INNER), lambda b, c: (row(b, c), v_blk)),
            pl.BlockSpec((L, INNER), lambda b, c: (row(b, c), v_blk + 1)),
            pl.BlockSpec((L, INNER), lambda b, c: (row(b, c), v_blk + 2)),
            pl.BlockSpec((L, LANES), lambda b, c: (row(b, c), 0)),
            pl.BlockSpec((1, LANES), lambda b, c: (0, 0)),
            pl.BlockSpec((1, INNER), lambda b, c: (0, 0)),
        ],
        out_specs=(
            pl.BlockSpec((L, INNER), lambda b, c: (row(b, c), 0)),
            pl.BlockSpec((1, H, DQK, DV), lambda b, c: (b, 0, 0, 0)),
            pl.BlockSpec((1, 1, QK), lambda b, c: (b, 0, 0)),
            pl.BlockSpec((1, 1, LANES), lambda b, c: (b, 0, 0)),
        ),
        compiler_params=_params(2),
        name="mlstm_prompt",
    )(p, p, p, p, p, gates, bias, gh)


SC_ROWS = 16


def _mlstm_sample_kernel(p_ref, gt_ref, bias_ref, gh_ref, n_ref, m_ref, v_ref, c_ref, *refs, H, DQK, DV, bt, layer,
                         first):
    (y_ref, c_out_ref, n_out_ref, m_out_ref, qkt_ref, sct_ref, hq_ref, a_ref, bc_ref) = refs[-9:]
    if first:
        for other in range(c_out_ref.shape[0]):
            if other != layer:
                c_out_ref[other] = jnp.zeros(c_out_ref.shape[1:], F32)
        c_out_ref = c_out_ref.at[layer]
    step = pl.program_id(0)
    head = pl.program_id(1)
    DB = p_ref.shape[0]
    QK, INNER = H * DQK, H * DV
    scale = DQK ** -0.5

    @pl.when((step == 0) & (head == 0))
    def _():
        gates = gt_ref[...] + bias_ref[...]
        lane = lax.broadcasted_iota(jnp.int32, gates.shape, 1)
        ig = jnp.where(lane < H, gates, 0.0)
        lf = jnp.where(lane < H, pltpu.roll(_log_sigmoid(gates), LANES - H, axis=1), 0.0)
        m_prev = m_ref[...]
        m_t = jnp.maximum(lf + m_prev, ig)
        w_inter = jnp.exp(lf + m_prev - m_t)
        wa = jnp.exp(ig - m_t)
        q = p_ref[:, 0:QK].astype(F32)
        k = p_ref[:, QK:2 * QK].astype(F32)
        n_prev = n_ref[...]
        qk = jnp.zeros_like(gates)
        qn = jnp.zeros_like(gates)
        for h in range(H):
            qs = slice(h * DQK, (h + 1) * DQK)
            qk = jnp.where(lane == h, jnp.sum(q[:, qs] * k[:, qs], axis=1, keepdims=True) * scale, qk)
            qn = jnp.where(lane == h, jnp.sum(q[:, qs] * n_prev[:, qs], axis=1, keepdims=True), qn)
            n_out_ref[:, qs] = w_inter[:, h:h + 1] * n_prev[:, qs] + (wa[:, h:h + 1] * scale) * k[:, qs]
        w_intra = wa * qk
        den = w_inter * qn + w_intra
        inv = 1.0 / jnp.maximum(jnp.abs(den), jnp.exp(-m_t))
        a_ref[...] = w_inter * inv
        bc_ref[...] = w_intra * inv
        m_out_ref[...] = jnp.where(lane < H, m_t, 0.0)
        qkt_ref[...] = p_ref[:, 0:2 * QK].astype(F32).T.astype(BF16)
        sc = jnp.where(lane < H, w_inter, pltpu.roll(wa * scale, H, axis=1))
        sct_ref[...] = jnp.where(lane < 2 * H, sc, 0.0).T[0:SC_ROWS, :]

    r = lax.broadcasted_iota(jnp.int32, (DB, LANES), 0)
    srow = lax.broadcasted_iota(jnp.int32, (SC_ROWS, LANES), 0)
    trow = lax.broadcasted_iota(jnp.int32, (bt, DV), 0)
    qt = qkt_ref[pl.ds(pl.multiple_of(head * DQK, DQK), DQK), :]
    kt = qkt_ref[pl.ds(pl.multiple_of(QK + head * DQK, DQK), DQK), :]
    s1, s2, s3 = _split3(sct_ref[...])
    tile = jnp.zeros((bt, DV), F32)
    for i in range(bt):
        onehot = jnp.where(r == step * bt + i, 1.0, 0.0).astype(BF16)
        qcol = _dot(qt, onehot)
        kcol = _dot(kt, onehot)
        scal = _dot(s1, onehot) + _dot(s2, onehot) + _dot(s3, onehot)
        decay = jnp.sum(jnp.where(srow == head, scal, 0.0), axis=0, keepdims=True)
        wsc = jnp.sum(jnp.where(srow == H + head, scal, 0.0), axis=0, keepdims=True)
        hrow = []
        for t in range(DV // LANES):
            cs = slice(t * LANES, (t + 1) * LANES)
            c_prev = c_ref[i, 0, :, cs]
            hrow.append(jnp.sum(qcol * c_prev, axis=0, keepdims=True))
            c_out_ref[i, 0, :, cs] = decay * c_prev + kcol * (wsc * v_ref[i:i + 1, cs])
        tile = jnp.where(trow == i, jnp.concatenate(hrow, axis=1), tile)
    hq_ref[head, pl.ds(pl.multiple_of(step * bt, bt), bt), :] = tile

    @pl.when((step == pl.num_programs(0) - 1) & (head == H - 1))
    def _():
        for h in range(H):
            vs = slice(h * DV, (h + 1) * DV)
            v = p_ref[:, 2 * QK + h * DV:2 * QK + (h + 1) * DV].astype(F32)
            hid = a_ref[:, h:h + 1] * hq_ref[h] + bc_ref[:, h:h + 1] * v
            hid = hid * lax.rsqrt(jnp.mean(hid * hid, axis=1, keepdims=True) + EPS)
            og = p_ref[:, 2 * QK + INNER + h * DV:2 * QK + INNER + (h + 1) * DV].astype(F32)
            zg = p_ref[:, 2 * QK + 2 * INNER + h * DV:2 * QK + 2 * INNER + (h + 1) * DV].astype(F32)
            y_ref[:, vs] = (hid * gh_ref[:, vs] * _sigmoid(og) * (zg * _sigmoid(zg))).astype(BF16)


def _mlstm_sample(p, gates, bias, gh, c_all, layer, c_out_prev, n_in, m_in, *, H, DQK, DV, bt):
    DB = p.shape[0]
    QK, INNER = H * DQK, H * DV
    assert 2 * H <= SC_ROWS and DB % bt == 0
    first = c_out_prev is None
    kern = functools.partial(_mlstm_sample_kernel, H=H, DQK=DQK, DV=DV, bt=bt, layer=layer, first=first)
    full = lambda shape: pl.BlockSpec(shape, lambda s, h: (0,) * len(shape))
    c_spec = pl.BlockSpec((None, bt, 1, DQK, DV), lambda s, h: (layer, s, h, 0, 0))
    c_out_spec = pl.BlockSpec((c_all.shape[0], bt, 1, DQK, DV), lambda s, h: (0, s, h, 0, 0)) if first else c_spec
    v_heads = p[:, 2 * QK:2 * QK + INNER].reshape(DB, H, DV).transpose(1, 0, 2)
    args = [p, gates, bias, gh, n_in, m_in, v_heads, c_all]
    in_specs = [full(p.shape), full((DB, LANES)), full((1, LANES)), full((1, INNER)),
                full((DB, QK)), full((DB, LANES)),
                pl.BlockSpec((None, bt, DV), lambda s, h: (h, s, 0)), c_spec]
    aliases = {}
    if c_out_prev is not None:
        aliases = {len(args): 1}
        args.append(c_out_prev)
        in_specs.append(pl.BlockSpec(memory_space=pl.ANY))
    return pl.pallas_call(
        kern,
        out_shape=(jax.ShapeDtypeStruct((DB, INNER), BF16),
                   jax.ShapeDtypeStruct(c_all.shape, F32),
                   jax.ShapeDtypeStruct((DB, QK), F32),
                   jax.ShapeDtypeStruct((DB, LANES), F32)),
        grid=(DB // bt, H),
        in_specs=in_specs,
        out_specs=(full((DB, INNER)), c_out_spec, full((DB, QK)), full((DB, LANES))),
        scratch_shapes=[pltpu.VMEM((2 * QK, DB), BF16), pltpu.VMEM((SC_ROWS, DB), F32),
                        pltpu.VMEM((H, DB, DV), F32), pltpu.VMEM((DB, LANES), F32), pltpu.VMEM((DB, LANES), F32)],
        input_output_aliases=aliases,
        compiler_params=_params(2),
        name="mlstm_sample",
    )(*args)


def _attn_prompt_kernel(q_ref, kc_ref, kp_ref, vc_ref, vp_ref, o_ref, lse_ref, *, nh, dh, span):
    n = pl.program_id(2)
    blk = q_ref.shape[0]
    scale = dh ** -0.5
    iq = lax.broadcasted_iota(jnp.int32, (blk, blk), 0)
    ik = lax.broadcasted_iota(jnp.int32, (blk, blk), 1)
    mask_cur = (ik <= iq) & (iq - ik <= span)
    mask_prev = blk + iq - ik <= jnp.where(n > 0, span, -1)
    lane = lax.broadcasted_iota(jnp.int32, (blk, LANES), 1)
    lse_all = jnp.zeros((blk, LANES), F32)
    heads = [slice(h * dh, (h + 1) * dh) for h in range(nh)]
    scores = [(_dot_nt(q_ref[:, hs], kc_ref[:, hs]), _dot_nt(q_ref[:, hs], kp_ref[:, hs])) for hs in heads]
    probs = []
    for h, (s_cur, s_prev) in enumerate(scores):
        s_cur = jnp.where(mask_cur, s_cur * scale, -jnp.inf)
        s_prev = jnp.where(mask_prev, s_prev * scale, -jnp.inf)
        mx = jnp.maximum(jnp.max(s_cur, axis=1, keepdims=True), jnp.max(s_prev, axis=1, keepdims=True))
        p_cur = jnp.exp(s_cur - mx)
        p_prev = jnp.exp(s_prev - mx)
        den = jnp.sum(p_cur, axis=1, keepdims=True) + jnp.sum(p_prev, axis=1, keepdims=True)
        inv = 1.0 / den
        probs.append(((p_cur * inv).astype(BF16), (p_prev * inv).astype(BF16)))
        lse_all = jnp.where(lane == h, mx + jnp.log(den), lse_all)
    for hs, (p_cur, p_prev) in zip(heads, probs):
        o_ref[:, hs] = (_dot(p_cur, vc_ref[:, hs]) + _dot(p_prev, vp_ref[:, hs])).astype(BF16)
    lse_ref[...] = lse_all


def _attn_prompt(qkv, *, nh, dh, win):
    B, dil, Ls, _ = qkv.shape
    GW = nh * dh
    nb = Ls // A_BLK
    kern = functools.partial(_attn_prompt_kernel, nh=nh, dh=dh, span=win // dil)

    def spec(col, prev):
        if prev:
            return pl.BlockSpec((None, None, A_BLK, GW), lambda b, r, n: (b, r, jnp.maximum(n - 1, 0), col))
        return pl.BlockSpec((None, None, A_BLK, GW), lambda b, r, n: (b, r, n, col))

    return pl.pallas_call(
        kern,
        out_shape=(jax.ShapeDtypeStruct((B, dil, Ls, GW), BF16),
                   jax.ShapeDtypeStruct((B, dil, Ls, LANES), F32)),
        grid=(B, dil, nb),
        in_specs=[spec(0, False), spec(1, False), spec(1, True), spec(2, False), spec(2, True)],
        out_specs=(pl.BlockSpec((None, None, A_BLK, GW), lambda b, r, n: (b, r, n, 0)),
                   pl.BlockSpec((None, None, A_BLK, LANES), lambda b, r, n: (b, r, n, 0))),
        compiler_params=_params(3),
        name=f"attn_prompt_d{dil}",
    )(qkv, qkv, qkv, qkv, qkv)


def _attn_sample_kernel(*refs, ng, nh, dh):
    qkv_refs, z_ref = refs[:ng], refs[ng]
    cache_refs, y_ref = refs[ng + 1:3 * ng + 1], refs[3 * ng + 1]
    bt = z_ref.shape[0]
    scale = dh ** -0.5
    pad = jnp.zeros((nh, dh), F32)
    rows16 = lambda t: jnp.concatenate([t, pad], axis=0)
    cases = [(i, g) for i in range(bt) for g in range(ng)]
    flat = lambda ref, i: ref[i].reshape(ref.shape[1] * nh, dh).astype(BF16)
    qs = {(i, g): rows16(qkv_refs[g][i, 0]) for i, g in cases}
    scores = {(i, g): _dot_nt(qs[i, g].astype(BF16), flat(cache_refs[2 * g], i)) for i, g in cases}
    probs = {}
    for i, g in cases:
        q, kn = qs[i, g], rows16(qkv_refs[g][i, 1])
        span = cache_refs[2 * g].shape[1]
        head = lax.broadcasted_iota(jnp.int32, (2 * nh, span * nh), 0)
        key_head = lax.broadcasted_iota(jnp.int32, (2 * nh, span * nh), 1) & (nh - 1)
        s = jnp.where(key_head == head, scores[i, g] * scale, -jnp.inf)
        s0 = jnp.sum(q * kn, axis=1, keepdims=True) * scale
        mx = jnp.maximum(jnp.max(s, axis=1, keepdims=True), s0)
        pr = jnp.exp(s - mx)
        p0 = jnp.exp(s0 - mx)
        den = jnp.sum(pr, axis=1, keepdims=True) + p0
        inv = 1.0 / den
        probs[i, g] = ((pr * inv).astype(BF16), p0 * inv, mx + jnp.log(den))
    outs = {(i, g): _dot(probs[i, g][0], flat(cache_refs[2 * g + 1], i)) for i, g in cases}
    for i in range(bt):
        lses = [probs[i, g][2] for g in range(ng)]
        mxl = functools.reduce(jnp.maximum, lses)
        es = [jnp.exp(l - mxl) for l in lses]
        inv = 1.0 / functools.reduce(jnp.add, es)
        terms = [(es[g] * inv) * (outs[i, g] + probs[i, g][1] * rows16(qkv_refs[g][i, 2])) for g in range(ng)]
        z = z_ref[i]
        y_ref[i] = functools.reduce(jnp.add, terms)[0:nh] * (z * _sigmoid(z))


def _attn_sample(qkvs, z, caches, layer, *, nh, dh, bt):
    DB = z.shape[0]
    ng = len(qkvs)
    assert nh & (nh - 1) == 0
    views, specs = [], []
    for g, (win, dil) in enumerate(A_GROUPS):
        span = win // dil
        for c in caches[2 * g:2 * g + 2]:
            assert c.shape[2] == win, "decode path expects a full window of cached rows"
            views.append(c.reshape(c.shape[0], DB, span, dil, nh, dh))
            specs.append(pl.BlockSpec((None, bt, span, None, nh, dh), lambda s: (layer, s, 0, 0, 0, 0)))
    kern = functools.partial(_attn_sample_kernel, ng=ng, nh=nh, dh=dh)
    return pl.pallas_call(
        kern,
        out_shape=jax.ShapeDtypeStruct((DB, nh, dh), F32),
        grid=(DB // bt,),
        in_specs=[pl.BlockSpec((bt, 3, nh, dh), lambda s: (s, 0, 0, 0))] * ng
        + [pl.BlockSpec((bt, nh, dh), lambda s: (s, 0, 0))] + specs,
        out_specs=pl.BlockSpec((bt, nh, dh), lambda s: (s, 0, 0)),
        compiler_params=_params(1),
        name="attn_sample",
    )(*qkvs, z, *views)


def _rope_tables(pos, dh):
    half = dh // 2
    inv = ROPE_THETA ** (-jnp.arange(half, dtype=F32) / half)
    ang = pos.astype(F32)[:, None] * inv[None, :]
    cos, sin = jnp.cos(ang), jnp.sin(ang)
    return jnp.concatenate([cos, cos], axis=1), jnp.concatenate([-sin, sin], axis=1)


def _row_tile(t, cap):
    tm = min(t, cap)
    while t % tm:
        tm //= 2
    return tm


def kernel(x_prompt, x_sample, state_C, state_n, state_m, cache_k1, cache_v1, cache_k2, cache_v2, cache_k3, cache_v3, ln_g, m_w_in, m_b_i, m_b_f, m_g_h, m_w_out, a_w_in, a_g_q, a_g_k, a_w_out):
    B, S, D = x_prompt.shape
    DB, DS, _ = x_sample.shape
    assert DS == 1
    depth = ln_g.shape[0]
    H = m_b_i.shape[1]
    INNER = m_w_out.shape[1]
    QK = (m_w_in.shape[2] - 3 * INNER - 2 * H) // 2
    DQK, DV = QK // H, INNER // H
    ng, dh = a_g_q.shape[1], a_g_q.shape[2]
    GW = a_w_out.shape[1]
    nh = GW // dh
    assert ng == len(A_GROUPS) and dh == LANES
    caches = (cache_k1, cache_v1, cache_k2, cache_v2, cache_k3, cache_v3)

    xp = x_prompt.reshape(B * S, D)
    xs = x_sample.reshape(DB, D)
    tm_p = _row_tile(S, 1024)
    tm_s = DB
    cos_p, sin_p = _rope_tables(jnp.arange(S), dh)
    cos_s, sin_s = _rope_tables(jnp.full((DB,), PAST_LEN), dh)
    ones = jnp.ones((1, (ng + 1) * GW), F32)
    dils = tuple(dil for _, dil in A_GROUPS)

    mp, ms, ap, as_ = [], [], [], []
    c_s = None
    for i in range(depth):
        j = i // 2
        g = ln_g[i].reshape(1, D)
        if i % 2 == 0:
            w_main = m_w_in[j][:, :2 * QK + 3 * INNER].astype(BF16)
            w_gate = jnp.pad(m_w_in[j][:, 2 * QK + 3 * INNER:], ((0, 0), (0, LANES - 2 * H))).astype(BF16)
            w_out = m_w_out[j].astype(BF16)
            bias = jnp.pad(jnp.concatenate([m_b_i[j], m_b_f[j]]), (0, LANES - 2 * H)).reshape(1, LANES)
            gh = m_g_h[j].reshape(1, INNER)

            p, gates = _proj_mlstm(xp, g, w_main, w_gate, tm=tm_p, tn=1024, out_dtype=BF16)
            y, c_p, n_p, m_p = _mlstm_prompt(p, gates, bias, gh, B=B, S=S, H=H, DQK=DQK, DV=DV)
            xp = _outproj(y, w_out, xp, tm=_row_tile(B * S, 512))
            mp.append((c_p, n_p.reshape(B, H, DQK), m_p[:, 0, :H]))

            p, gates = _proj_mlstm(xs, g, w_main, w_gate, tm=tm_s, tn=1024, out_dtype=F32)
            m_in = jnp.pad(state_m[j], ((0, 0), (0, LANES - H)))
            y, c_s, n_s, m_s = _mlstm_sample(p, gates, bias, gh, state_C, j, c_s, state_n[j].reshape(DB, QK), m_in,
                                             H=H, DQK=DQK, DV=DV, bt=8)
            xs = _outproj(y, w_out, xs, tm=tm_s)
            ms.append((n_s.reshape(DB, H, DQK), m_s[:, :H]))
        else:
            w_in = a_w_in[j].astype(BF16)
            w_out = a_w_out[j].astype(BF16)
            gain = jnp.concatenate([jnp.tile(a_g_q[j], (1, nh)).reshape(1, ng * GW),
                                    jnp.tile(a_g_k[j], (1, nh)).reshape(1, ng * GW), ones], axis=1)

            qkvs, z = _proj_attn(xp, g, w_in, gain, cos_p, sin_p, B=B, tm=tm_p, tn=GW, ng=ng, dh=dh, dils=dils,
                                 out_dtype=BF16)
            os_, ls_ = zip(*[_attn_prompt(qkvs[gi], nh=nh, dh=dh, win=win) for gi, (win, _) in enumerate(A_GROUPS)])
            xp = _outproj_comb(os_, ls_, z, w_out, xp, B=B, tm=_row_tile(S, 512), nh=nh, dh=dh, dils=dils)
            rows = []
            for gi, (win, dil) in enumerate(A_GROUPS):
                n_keep = min(win, S) // dil
                tail = qkvs[gi][:, :, S // dil - n_keep:, GW:].astype(F32)
                tail = tail.transpose(0, 2, 1, 3).reshape(B, n_keep * dil, 2, nh, dh)
                rows += [tail[:, :, 0], tail[:, :, 1]]
            ap.append(rows)

            qkvs, z = _proj_attn(xs, g, w_in, gain, cos_s, sin_s, B=1, tm=tm_s, tn=GW, ng=ng, dh=dh, dils=(1,) * ng,
                                 out_dtype=F32)
            qkvs = [t.reshape(DB, 3, nh, dh) for t in qkvs]
            y = _attn_sample(qkvs, z.reshape(DB, nh, dh), caches, j, nh=nh, dh=dh, bt=4)
            xs = _outproj(y.reshape(DB, GW), w_out, xs, tm=tm_s)
            as_.append([t[:, c].reshape(DB, 1, nh, dh) for t in qkvs for c in (1, 2)])

    stack = lambda items, k: jnp.stack([it[k] for it in items])
    return ((xp.reshape(B, S, D), xs.reshape(DB, DS, D), stack(mp, 0), stack(mp, 1), stack(mp, 2))
            + tuple(stack(ap, k) for k in range(2 * ng))
            + (c_s, stack(ms, 0), stack(ms, 1))
            + tuple(stack(as_, k) for k in range(2 * ng)))
```

```python
import functools

import jax
import jax.numpy as jnp
from jax import lax
from jax.experimental import pallas as pl
from jax.experimental.pallas import tpu as pltpu

F32 = jnp.float32
BF16 = jnp.bfloat16
EPS = 1e-6
LANES = 128
VMEM_LIMIT = 56 * 1024 * 1024

A_GROUPS = ((128, 1), (512, 4), (2048, 16))
PAST_LEN = 2048
ROPE_THETA = 10000.0
A_BLK = 128
A_STEP_BLOCKS = 4
M_BLOCK = 256
PROJ_UNIT_ROWS = 512


def _params(n_axes):
    return pltpu.CompilerParams(dimension_semantics=("arbitrary",) * n_axes, vmem_limit_bytes=VMEM_LIMIT)


def _sigmoid(x):
    return 0.5 * jnp.tanh(0.5 * x) + 0.5


def _log_sigmoid(x):
    return jnp.minimum(x, 0.0) - jnp.log1p(jnp.exp(-jnp.abs(x)))


def _rms_rows(x, g):
    ms = jnp.mean(x * x, axis=-1, keepdims=True)
    return x * lax.rsqrt(ms + EPS) * g


def _split3(x):
    x1 = x.astype(BF16)
    r1 = x - x1.astype(F32)
    x2 = r1.astype(BF16)
    x3 = (r1 - x2.astype(F32)).astype(BF16)
    return x1, x2, x3


def _dot(a, b):
    return jnp.dot(a, b, preferred_element_type=F32)


def _dot_nt(a, b):
    return lax.dot_general(a, b, (((1,), (1,)), ((), ())), preferred_element_type=F32)


def _dot_tn(a, b):
    return lax.dot_general(a, b, (((0,), (0,)), ((), ())), preferred_element_type=F32)


def _proj_mlstm_kernel(x_ref, g_ref, w_ref, wg_ref, o_ref, gate_ref, xn_ref):
    @pl.when(pl.program_id(1) == 0)
    def _():
        xn = _rms_rows(x_ref[...], g_ref[...]).astype(BF16)
        xn_ref[...] = xn
        gate_ref[...] = _dot(xn, wg_ref[...])

    o_ref[...] = _dot(xn_ref[...], w_ref[...]).astype(o_ref.dtype)


def _proj_mlstm(x, g, w, wg, *, tm, tn, out_dtype):
    T, D = x.shape
    P = w.shape[1]
    return pl.pallas_call(
        _proj_mlstm_kernel,
        out_shape=(jax.ShapeDtypeStruct((T, P), out_dtype), jax.ShapeDtypeStruct((T, LANES), F32)),
        grid=(T // tm, P // tn),
        in_specs=[
            pl.BlockSpec((tm, D), lambda i, j: (i, 0)),
            pl.BlockSpec((1, D), lambda i, j: (0, 0)),
            pl.BlockSpec((D, tn), lambda i, j: (0, j)),
            pl.BlockSpec((D, LANES), lambda i, j: (0, 0)),
        ],
        out_specs=(
            pl.BlockSpec((tm, tn), lambda i, j: (i, j)),
            pl.BlockSpec((tm, LANES), lambda i, j: (i, 0)),
        ),
        scratch_shapes=[pltpu.VMEM((tm, D), BF16)],
        compiler_params=_params(2),
        name="proj_mlstm",
    )(x, g, w, wg)


def _proj_attn_kernel(x_ref, g_ref, w_ref, gain_ref, cos_ref, sin_ref, *refs, ng, dh, sub, dils):
    outs, z_ref, xn_ref, acc_ref = refs[:ng], refs[ng], refs[ng + 1], refs[ng + 2]
    j = pl.program_id(1)
    n_slabs, tm, _ = acc_ref.shape

    @pl.when(j == 0)
    def _():
        xn_ref[...] = _rms_rows(x_ref[...], g_ref[...]).astype(BF16)

    @pl.when(j == 3 * ng)
    def _():
        z_ref[...] = _dot(xn_ref[...], w_ref[...]).astype(z_ref.dtype)

    def project(p):
        acc = _dot(xn_ref[...], w_ref[:, 2 * p * dh:2 * (p + 1) * dh])
        acc_ref[2 * p] = acc[:, :dh]
        acc_ref[2 * p + 1] = acc[:, dh:]

    def tile(g, dil, rot):
        per_res = tm // dil
        unit_rows = min(tm, PROJ_UNIT_ROWS)
        units = []
        for u in range(tm // unit_rows):
            lo = u * unit_rows
            if per_res >= unit_rows:
                units.append([(lo // per_res, lo % per_res, unit_rows)])
            else:
                units.append([(lo // per_res + k, 0, per_res) for k in range(unit_rows // per_res)])

        def gather(ref, seg, *lead):
            r, i0, n = seg
            rows = pl.ds(i0, n) if dil == 1 else pl.ds(r + dil * i0, n, stride=dil)
            return ref[(*lead, rows, slice(None))]

        if rot:
            kk = lax.broadcasted_iota(jnp.int32, (2 * dh, 2 * dh), 0)
            cc = lax.broadcasted_iota(jnp.int32, (2 * dh, 2 * dh), 1)
            head_sum = jnp.where(_head_of(kk, dh) == _head_of(cc, dh), 1.0, 0.0).astype(BF16)
            half_swap = jnp.where(kk == (cc ^ (dh // 2)), 1.0, 0.0).astype(BF16)
        def finish(p):
            vals = [jnp.concatenate(
                [jnp.concatenate([gather(acc_ref, sg, 2 * p + t) for sg in segs], axis=0) for t in range(2)], axis=1)
                for segs in units]
            if rot:
                tabs = []
                for segs in units:
                    cos = jnp.concatenate([gather(cos_ref, sg) for sg in segs], axis=0)
                    sin = jnp.concatenate([gather(sin_ref, sg) for sg in segs], axis=0)
                    tabs.append((jnp.concatenate([cos, cos], axis=1), jnp.concatenate([sin, sin], axis=1)))
                sums = [_dot((a * a).astype(BF16), head_sum) for a in vals]
                gain = gain_ref[:, 2 * p * dh:2 * (p + 1) * dh]
                vals = [a * lax.rsqrt(ss * (1.0 / dh) + EPS) * gain for a, ss in zip(vals, sums)]
                his = [a.astype(BF16) for a in vals]
                los = [(a - hi.astype(F32)).astype(BF16) for a, hi in zip(vals, his)]
                swapped = [_dot(hi, half_swap) + _dot(lo, half_swap) for hi, lo in zip(his, los)]
                vals = [a * cos + sw * sin for a, sw, (cos, sin) in zip(vals, swapped, tabs)]
            for segs, a in zip(units, vals):
                a = a.astype(outs[g].dtype)
                row = 0
                for r, i0, n in segs:
                    outs[g][r, i0:i0 + n, 2 * p * dh:2 * (p + 1) * dh] = a[row:row + n]
                    row += n

        n_pairs = n_slabs // 2
        project(0)
        for p in range(n_pairs):
            if p + 1 < n_pairs:
                project(p + 1)
            finish(p)

    for g, dil in enumerate(dils):
        @pl.when((j < 2 * ng) & (lax.rem(j, ng) == g))
        def _(g=g, dil=dil):
            tile(g, dil, True)

        @pl.when((j >= 2 * ng) & (j < 3 * ng) & (lax.rem(j, ng) == g))
        def _(g=g, dil=dil):
            tile(g, dil, False)


def _proj_attn(x, g, w, gain, cos, sin, *, B, tm, tn, ng, dh, dils, out_dtype):
    T, D = x.shape
    S = T // B
    tiles = S // tm
    pos_blocks = cos.shape[0] // tm
    assert w.shape[1] == (3 * ng + 1) * tn and all(tm % d == 0 for d in dils)
    kern = functools.partial(_proj_attn_kernel, ng=ng, dh=dh, sub=2 * dh, dils=dils)

    def group_spec(gi, dil):
        return pl.BlockSpec((None, dil, tm // dil, tn),
                            lambda i, j: (i // tiles, 0, i % tiles, jnp.clip(jnp.maximum(j - gi, 0) // ng, 0, 2)))

    *qkvs, z = pl.pallas_call(
        kern,
        out_shape=tuple(jax.ShapeDtypeStruct((B, dil, S // dil, 3 * tn), out_dtype) for dil in dils)
        + (jax.ShapeDtypeStruct((T, tn), out_dtype),),
        grid=(T // tm, 3 * ng + 1),
        in_specs=[
            pl.BlockSpec((tm, D), lambda i, j: (i, 0)),
            pl.BlockSpec((1, D), lambda i, j: (0, 0)),
            pl.BlockSpec((D, tn), lambda i, j: (0, j)),
            pl.BlockSpec((1, tn), lambda i, j: (0, j)),
            pl.BlockSpec((tm, dh), lambda i, j: (i % pos_blocks, 0)),
            pl.BlockSpec((tm, dh), lambda i, j: (i % pos_blocks, 0)),
        ],
        out_specs=tuple(group_spec(gi, dil) for gi, dil in enumerate(dils))
        + (pl.BlockSpec((tm, tn), lambda i, j: (i, 0)),),
        scratch_shapes=[pltpu.VMEM((tm, D), BF16), pltpu.VMEM((tn // dh, tm, dh), F32)],
        compiler_params=_params(2),
        name="proj_attn",
    )(x, g, w, gain, cos, sin)
    return qkvs, z


def _outproj_kernel(y_ref, w_ref, r_ref, o_ref):
    o_ref[...] = r_ref[...] + _dot(y_ref[...].astype(BF16), w_ref[...])


def _outproj(y, w, resid, *, tm):
    T, E = y.shape
    D = w.shape[1]
    return pl.pallas_call(
        _outproj_kernel,
        out_shape=jax.ShapeDtypeStruct((T, D), F32),
        grid=(T // tm,),
        in_specs=[
            pl.BlockSpec((tm, E), lambda i: (i, 0)),
            pl.BlockSpec((E, D), lambda i: (0, 0)),
            pl.BlockSpec((tm, D), lambda i: (i, 0)),
        ],
        out_specs=pl.BlockSpec((tm, D), lambda i: (i, 0)),
        compiler_params=_params(1),
        name="outproj",
    )(y, w, resid)


def _head_of(col, dh):
    return lax.shift_right_logical(col, dh.bit_length() - 1)


def _head_expander(nh, dh):
    r = lax.broadcasted_iota(jnp.int32, (LANES, nh * dh), 0)
    c = lax.broadcasted_iota(jnp.int32, (LANES, nh * dh), 1)
    return jnp.where(_head_of(c, dh) == r, 1.0, 0.0).astype(BF16)


def _outproj_comb_kernel(*refs, ng, nh, dh, dils):
    o_refs, l_refs = refs[:ng], refs[ng:2 * ng]
    z_ref, w_ref, r_ref, out_ref, ot_ref, lt_ref = refs[2 * ng:]
    tm = out_ref.shape[0]
    for g, dil in enumerate(dils):
        for r in range(dil):
            rows = slice(None) if dil == 1 else pl.ds(r, tm // dil, stride=dil)
            lt_ref[g, rows, :] = l_refs[g][r]
            for s in range(nh):
                ot_ref[g, s, rows, :] = o_refs[g][r, :, s * dh:(s + 1) * dh].astype(F32)
    lses = [lt_ref[g] for g in range(ng)]
    mx = functools.reduce(jnp.maximum, lses)
    es = [jnp.exp(l - mx) for l in lses]
    inv = 1.0 / functools.reduce(jnp.add, es)
    expand = _head_expander(nh, dh)
    y = None
    for g in range(ng):
        w_hi, w_lo, _ = _split3(es[g] * inv)
        wide = _dot(w_hi, expand) + _dot(w_lo, expand)
        term = wide * jnp.concatenate([ot_ref[g, s] for s in range(nh)], axis=1)
        y = term if y is None else y + term
    z = z_ref[...].astype(F32)
    y = y * (z * _sigmoid(z))
    out_ref[...] = r_ref[...] + _dot(y.astype(BF16), w_ref[...])


def _outproj_comb(os_, ls_, z, w, resid, *, B, tm, nh, dh, dils):
    T, D = resid.shape
    E = w.shape[0]
    ng = len(dils)
    tiles = T // B // tm
    kern = functools.partial(_outproj_comb_kernel, ng=ng, nh=nh, dh=dh, dils=dils)
    res_spec = lambda dil, width: pl.BlockSpec((None, dil, tm // dil, width), lambda i: (i // tiles, 0, i % tiles, 0))
    return pl.pallas_call(
        kern,
        out_shape=jax.ShapeDtypeStruct((T, D), F32),
        grid=(T // tm,),
        in_specs=[res_spec(dil, E) for dil in dils] + [res_spec(dil, LANES) for dil in dils]
        + [pl.BlockSpec((tm, E), lambda i: (i, 0)),
           pl.BlockSpec((E, D), lambda i: (0, 0)),
           pl.BlockSpec((tm, D), lambda i: (i, 0))],
        out_specs=pl.BlockSpec((tm, D), lambda i: (i, 0)),
        scratch_shapes=[pltpu.VMEM((ng, nh, tm, dh), F32), pltpu.VMEM((ng, tm, LANES), F32)],
        compiler_params=_params(1),
        name="outproj_comb",
    )(*os_, *ls_, z, w, resid)


def _mlstm_prompt_kernel(q_ref, k_ref, v_ref, o_ref, z_ref, gt_ref, bias_ref, gh_ref,
                         y_ref, c_ref, n_ref, m_ref, *, H, DQK, DV):
    L = q_ref.shape[0]
    scale = DQK ** -0.5

    @pl.when(pl.program_id(1) == 0)
    def _():
        c_ref[...] = jnp.zeros_like(c_ref)
        n_ref[...] = jnp.zeros_like(n_ref)
        m_ref[...] = jnp.zeros_like(m_ref)

    gates = gt_ref[...] + bias_ref[...]
    lane = lax.broadcasted_iota(jnp.int32, gates.shape, 1)
    x = jnp.where(lane < H, gates, _log_sigmoid(gates))
    row = lax.broadcasted_iota(jnp.int32, (L, L), 0)
    col = lax.broadcasted_iota(jnp.int32, (L, L), 1)
    causal = col <= row
    tri = jnp.where(causal, 1.0, 0.0).astype(BF16)
    x1, x2, x3 = _split3(x)
    cum = _dot(tri, x1) + _dot(tri, x2) + _dot(tri, x3)
    xt = x.T
    cumt = cum.T

    m_all = m_ref[0]
    lane1 = lax.broadcasted_iota(jnp.int32, m_all.shape, 1)
    qss = [slice(h * DQK, (h + 1) * DQK) for h in range(H)]
    vss = [slice(h * DV, (h + 1) * DV) for h in range(H)]
    st = [dict() for _ in range(H)]
    gh4 = 0.25 * gh_ref[...]

    def stage_a(h):
        st[h]["qk"] = _dot_nt(q_ref[:, qss[h]], k_ref[:, qss[h]])
        st[h]["qc"] = _dot(q_ref[:, qss[h]], c_ref[0, h].astype(BF16))

    def stage_b(h):
        qs = qss[h]
        ig_row = xt[h:h + 1, :]
        ig_col = x[:, h:h + 1]
        b_row = cumt[H + h:H + h + 1, :]
        b_col = cum[:, H + h:H + h + 1]
        g_tot = cumt[H + h:H + h + 1, L - 1:L]
        m_prev = m_all[:, h:h + 1]

        dlog = jnp.where(causal, b_col - b_row + ig_row, -jnp.inf)
        inter = b_col + m_prev
        m_t = jnp.maximum(inter, jnp.max(dlog, axis=1, keepdims=True))
        w_inter = jnp.exp(inter - m_t)
        w_intra = jnp.exp(dlog - m_t) * (st[h].pop("qk") * scale)
        n_prev = n_ref[0, :, qs]
        den = (w_inter * jnp.sum(q_ref[:, qs].astype(F32) * n_prev, axis=1, keepdims=True)
               + jnp.sum(w_intra, axis=1, keepdims=True))
        inv = 1.0 / jnp.maximum(jnp.abs(den), jnp.exp(-m_t))

        a_col = ig_col + g_tot - b_col
        m_new = jnp.maximum(g_tot + m_prev, jnp.max(a_col, axis=0, keepdims=True))
        decay = jnp.exp(g_tot + m_prev - m_new)
        kw = k_ref[:, qs].astype(F32) * (jnp.exp(a_col - m_new) * scale)
        n_ref[0, :, qs] = decay * n_prev + jnp.sum(kw, axis=0, keepdims=True)
        st[h].update(w_inter=w_inter, w_intra=w_intra.astype(BF16), inv=inv, decay=decay, kw=kw.astype(BF16),
                     m_new=m_new)

    def stage_c(h):
        st[h]["wv"] = _dot(st[h].pop("w_intra"), v_ref[:, vss[h]])
        st[h]["kv"] = _dot_tn(st[h].pop("kw"), v_ref[:, vss[h]])

    def stage_d(h):
        vs = vss[h]
        hid = (st[h]["w_inter"] * st[h]["qc"] + st[h]["wv"]) * st[h]["inv"]
        hid = hid * lax.rsqrt(jnp.mean(hid * hid, axis=1, keepdims=True) + EPS)
        og = o_ref[:, vs].astype(F32)
        zg = z_ref[:, vs].astype(F32)
        gate = (jnp.tanh(0.5 * og) + 1.0) * (jnp.tanh(0.5 * zg) + 1.0) * zg
        y_ref[:, vs] = (hid * gh4[:, vs] * gate).astype(BF16)
        c_ref[0, h] = st[h]["decay"] * c_ref[0, h] + st[h]["kv"]

    order = [("a", 0)]
    for h in range(H):
        if h + 1 < H:
            order.append(("a", h + 1))
        order += [("b", h), ("c", h)]
        if h > 0:
            order.append(("d", h - 1))
    order.append(("d", H - 1))
    stages = dict(a=stage_a, b=stage_b, c=stage_c, d=stage_d)
    for name, h in order:
        stages[name](h)
    m_next = m_all
    for h in range(H):
        m_next = jnp.where(lane1 == h, st[h]["m_new"], m_next)
    m_ref[0] = m_next


def _mlstm_prompt(p, gates, bias, gh, *, B, S, H, DQK, DV):
    L = min(M_BLOCK, S)
    nc = S // L
    QK, INNER = H * DQK, H * DV
    kern = functools.partial(_mlstm_prompt_kernel, H=H, DQK=DQK, DV=DV)
    row = lambda b, c: b * nc + c
    v_blk = 2 * QK // INNER
    return pl.pallas_call(
        kern,
        out_shape=(jax.ShapeDtypeStruct((B * S, INNER), BF16),
                   jax.ShapeDtypeStruct((B, H, DQK, DV), F32),
                   jax.ShapeDtypeStruct((B, 1, QK), F32),
                   jax.ShapeDtypeStruct((B, 1, LANES), F32)),
        grid=(B, nc),
        in_specs=[
            pl.BlockSpec((L, QK), lambda b, c: (row(b, c), 0)),
            pl.BlockSpec((L, QK), lambda b, c: (row(b, c), 1)),
            pl.BlockSpec((L, INNER), lambda b, c: (row(b, c), v_blk)),
            pl.BlockSpec((L, INNER), lambda b, c: (row(b, c), v_blk + 1)),
            pl.BlockSpec((L, INNER), lambda b, c: (row(b, c), v_blk + 2)),
            pl.BlockSpec((L, LANES), lambda b, c: (row(b, c), 0)),
            pl.BlockSpec((1, LANES), lambda b, c: (0, 0)),
            pl.BlockSpec((1, INNER), lambda b, c: (0, 0)),
        ],
        out_specs=(
            pl.BlockSpec((L, INNER), lambda b, c: (row(b, c), 0)),
            pl.BlockSpec((1, H, DQK, DV), lambda b, c: (b, 0, 0, 0)),
            pl.BlockSpec((1, 1, QK), lambda b, c: (b, 0, 0)),
            pl.BlockSpec((1, 1, LANES), lambda b, c: (b, 0, 0)),
        ),
        compiler_params=_params(2),
        name="mlstm_prompt",
    )(p, p, p, p, p, gates, bias, gh)


SC_ROWS = 16


def _mlstm_sample_kernel(p_ref, gt_ref, bias_ref, gh_ref, n_ref, m_ref, v_ref, c_ref, *refs, H, DQK, DV, bt, layer,
                         first):
    (y_ref, c_out_ref, n_out_ref, m_out_ref, qkt_ref, sct_ref, hq_ref, a_ref, bc_ref) = refs[-9:]
    if first:
        for other in range(c_out_ref.shape[0]):
            if other != layer:
                c_out_ref[other] = jnp.zeros(c_out_ref.shape[1:], F32)
        c_out_ref = c_out_ref.at[layer]
    step = pl.program_id(0)
    head = pl.program_id(1)
    DB = p_ref.shape[0]
    QK, INNER = H * DQK, H * DV
    scale = DQK ** -0.5

    @pl.when((step == 0) & (head == 0))
    def _():
        gates = gt_ref[...] + bias_ref[...]
        lane = lax.broadcasted_iota(jnp.int32, gates.shape, 1)
        ig = jnp.where(lane < H, gates, 0.0)
        lf = jnp.where(lane < H, pltpu.roll(_log_sigmoid(gates), LANES - H, axis=1), 0.0)
        m_prev = m_ref[...]
        m_t = jnp.maximum(lf + m_prev, ig)
        w_inter = jnp.exp(lf + m_prev - m_t)
        wa = jnp.exp(ig - m_t)
        q = p_ref[:, 0:QK].astype(F32)
        k = p_ref[:, QK:2 * QK].astype(F32)
        n_prev = n_ref[...]
        qk = jnp.zeros_like(gates)
        qn = jnp.zeros_like(gates)
        for h in range(H):
            qs = slice(h * DQK, (h + 1) * DQK)
            qk = jnp.where(lane == h, jnp.sum(q[:, qs] * k[:, qs], axis=1, keepdims=True) * scale, qk)
            qn = jnp.where(lane == h, jnp.sum(q[:, qs] * n_prev[:, qs], axis=1, keepdims=True), qn)
            n_out_ref[:, qs] = w_inter[:, h:h + 1] * n_prev[:, qs] + (wa[:, h:h + 1] * scale) * k[:, qs]
        w_intra = wa * qk
        den = w_inter * qn + w_intra
        inv = 1.0 / jnp.maximum(jnp.abs(den), jnp.exp(-m_t))
        a_ref[...] = w_inter * inv
        bc_ref[...] = w_intra * inv
        m_out_ref[...] = jnp.where(lane < H, m_t, 0.0)
        qkt_ref[...] = p_ref[:, 0:2 * QK].astype(F32).T.astype(BF16)
        sc = jnp.where(lane < H, w_inter, pltpu.roll(wa * scale, H, axis=1))
        sct_ref[...] = jnp.where(lane < 2 * H, sc, 0.0).T[0:SC_ROWS, :]

    r = lax.broadcasted_iota(jnp.int32, (DB, LANES), 0)
    srow = lax.broadcasted_iota(jnp.int32, (SC_ROWS, LANES), 0)
    trow = lax.broadcasted_iota(jnp.int32, (bt, DV), 0)
    qt = qkt_ref[pl.ds(pl.multiple_of(head * DQK, DQK), DQK), :]
    kt = qkt_ref[pl.ds(pl.multiple_of(QK + head * DQK, DQK), DQK), :]
    s1, s2, s3 = _split3(sct_ref[...])
    tile = jnp.zeros((bt, DV), F32)
    for i in range(bt):
        onehot = jnp.where(r == step * bt + i, 1.0, 0.0).astype(BF16)
        qcol = _dot(qt, onehot)
        kcol = _dot(kt, onehot)
        scal = _dot(s1, onehot) + _dot(s2, onehot) + _dot(s3, onehot)
        decay = jnp.sum(jnp.where(srow == head, scal, 0.0), axis=0, keepdims=True)
        wsc = jnp.sum(jnp.where(srow == H + head, scal, 0.0), axis=0, keepdims=True)
        hrow = []
        for t in range(DV // LANES):
            cs = slice(t * LANES, (t + 1) * LANES)
            c_prev = c_ref[i, 0, :, cs]
            hrow.append(jnp.sum(qcol * c_prev, axis=0, keepdims=True))
            c_out_ref[i, 0, :, cs] = decay * c_prev + kcol * (wsc * v_ref[i:i + 1, cs])
        tile = jnp.where(trow == i, jnp.concatenate(hrow, axis=1), tile)
    hq_ref[head, pl.ds(pl.multiple_of(step * bt, bt), bt), :] = tile

    @pl.when((step == pl.num_programs(0) - 1) & (head == H - 1))
    def _():
        for h in range(H):
            vs = slice(h * DV, (h + 1) * DV)
            v = p_ref[:, 2 * QK + h * DV:2 * QK + (h + 1) * DV].astype(F32)
            hid = a_ref[:, h:h + 1] * hq_ref[h] + bc_ref[:, h:h + 1] * v
            hid = hid * lax.rsqrt(jnp.mean(hid * hid, axis=1, keepdims=True) + EPS)
            og = p_ref[:, 2 * QK + INNER + h * DV:2 * QK + INNER + (h + 1) * DV].astype(F32)
            zg = p_ref[:, 2 * QK + 2 * INNER + h * DV:2 * QK + 2 * INNER + (h + 1) * DV].astype(F32)
            y_ref[:, vs] = (hid * gh_ref[:, vs] * _sigmoid(og) * (zg * _sigmoid(zg))).astype(BF16)


def _mlstm_sample(p, gates, bias, gh, c_all, layer, c_out_prev, n_in, m_in, *, H, DQK, DV, bt):
    DB = p.shape[0]
    QK, INNER = H * DQK, H * DV
    assert 2 * H <= SC_ROWS and DB % bt == 0
    first = c_out_prev is None
    kern = functools.partial(_mlstm_sample_kernel, H=H, DQK=DQK, DV=DV, bt=bt, layer=layer, first=first)
    full = lambda shape: pl.BlockSpec(shape, lambda s, h: (0,) * len(shape))
    c_spec = pl.BlockSpec((None, bt, 1, DQK, DV), lambda s, h: (layer, s, h, 0, 0))
    c_out_spec = pl.BlockSpec((c_all.shape[0], bt, 1, DQK, DV), lambda s, h: (0, s, h, 0, 0)) if first else c_spec
    v_heads = p[:, 2 * QK:2 * QK + INNER].reshape(DB, H, DV).transpose(1, 0, 2)
    args = [p, gates, bias, gh, n_in, m_in, v_heads, c_all]
    in_specs = [full(p.shape), full((DB, LANES)), full((1, LANES)), full((1, INNER)),
                full((DB, QK)), full((DB, LANES)),
                pl.BlockSpec((None, bt, DV), lambda s, h: (h, s, 0)), c_spec]
    aliases = {}
    if c_out_prev is not None:
        aliases = {len(args): 1}
        args.append(c_out_prev)
        in_specs.append(pl.BlockSpec(memory_space=pl.ANY))
    return pl.pallas_call(
        kern,
        out_shape=(jax.ShapeDtypeStruct((DB, INNER), BF16),
                   jax.ShapeDtypeStruct(c_all.shape, F32),
                   jax.ShapeDtypeStruct((DB, QK), F32),
                   jax.ShapeDtypeStruct((DB, LANES), F32)),
        grid=(DB // bt, H),
        in_specs=in_specs,
        out_specs=(full((DB, INNER)), c_out_spec, full((DB, QK)), full((DB, LANES))),
        scratch_shapes=[pltpu.VMEM((2 * QK, DB), BF16), pltpu.VMEM((SC_ROWS, DB), F32),
                        pltpu.VMEM((H, DB, DV), F32), pltpu.VMEM((DB, LANES), F32), pltpu.VMEM((DB, LANES), F32)],
        input_output_aliases=aliases,
        compiler_params=_params(2),
        name="mlstm_sample",
    )(*args)


def _attn_prompt_kernel(q_ref, k_ref, v_ref, o_ref, lse_ref, kp_ref, vp_ref, *, nh, dh, span, blk):
    n = pl.program_id(2)
    nq = q_ref.shape[0] // blk
    scale = dh ** -0.5

    @pl.when(n == 0)
    def _():
        kp_ref[...] = jnp.zeros_like(kp_ref)
        vp_ref[...] = jnp.zeros_like(vp_ref)

    iq = lax.broadcasted_iota(jnp.int32, (blk, blk), 0)
    ik = lax.broadcasted_iota(jnp.int32, (blk, blk), 1)
    mask_cur = (ik <= iq) & (iq - ik <= span)
    mask_prev = blk + iq - ik <= span
    mask_first = blk + iq - ik <= jnp.where(n > 0, span, -1)
    lane = lax.broadcasted_iota(jnp.int32, (blk, LANES), 1)
    heads = [slice(h * dh, (h + 1) * dh) for h in range(nh)]

    def rows(t):
        return slice(t * blk, (t + 1) * blk)

    def prev_block(ref, carry_ref, t, hs):
        return carry_ref[:, hs] if t == 0 else ref[rows(t - 1), hs]

    for t0 in range(0, nq, 2):
        units = [(t, h) for t in range(t0, min(t0 + 2, nq)) for h in range(nh)]
        scores = {(t, h): (_dot_nt(q_ref[rows(t), heads[h]], k_ref[rows(t), heads[h]]),
                           _dot_nt(q_ref[rows(t), heads[h]], prev_block(k_ref, kp_ref, t, heads[h])))
                  for t, h in units}
        probs = {}
        lse = {t: jnp.zeros((blk, LANES), F32) for t, _ in units}
        for t, h in units:
            s_cur, s_prev = scores[t, h]
            s_cur = jnp.where(mask_cur, s_cur * scale, -jnp.inf)
            s_prev = jnp.where(mask_first if t == 0 else mask_prev, s_prev * scale, -jnp.inf)
            mx = jnp.maximum(jnp.max(s_cur, axis=1, keepdims=True), jnp.max(s_prev, axis=1, keepdims=True))
            p_cur = jnp.exp(s_cur - mx)
            p_prev = jnp.exp(s_prev - mx)
            den = jnp.sum(p_cur, axis=1, keepdims=True) + jnp.sum(p_prev, axis=1, keepdims=True)
            inv = 1.0 / den
            probs[t, h] = ((p_cur * inv).astype(BF16), (p_prev * inv).astype(BF16))
            lse[t] = jnp.where(lane == h, mx + jnp.log(den), lse[t])
        for t, h in units:
            p_cur, p_prev = probs[t, h]
            o_ref[rows(t), heads[h]] = (_dot(p_cur, v_ref[rows(t), heads[h]])
                                        + _dot(p_prev, prev_block(v_ref, vp_ref, t, heads[h]))).astype(BF16)
        for t in lse:
            lse_ref[rows(t), :] = lse[t]
    kp_ref[...] = k_ref[rows(nq - 1), :]
    vp_ref[...] = v_ref[rows(nq - 1), :]


def _attn_prompt(qkv, *, nh, dh, win):
    B, dil, Ls, _ = qkv.shape
    GW = nh * dh
    rows = min(A_STEP_BLOCKS * A_BLK, Ls)
    assert Ls % rows == 0
    kern = functools.partial(_attn_prompt_kernel, nh=nh, dh=dh, span=win // dil, blk=A_BLK)
    spec = lambda width, col: pl.BlockSpec((None, None, rows, width), lambda b, r, n: (b, r, n, col))
    return pl.pallas_call(
        kern,
        out_shape=(jax.ShapeDtypeStruct((B, dil, Ls, GW), BF16),
                   jax.ShapeDtypeStruct((B, dil, Ls, LANES), F32)),
        grid=(B, dil, Ls // rows),
        in_specs=[spec(GW, 0), spec(GW, 1), spec(GW, 2)],
        out_specs=(spec(GW, 0), spec(LANES, 0)),
        scratch_shapes=[pltpu.VMEM((A_BLK, GW), BF16), pltpu.VMEM((A_BLK, GW), BF16)],
        compiler_params=_params(3),
        name=f"attn_prompt_d{dil}",
    )(qkv, qkv, qkv)


def _attn_sample_kernel(*refs, ng, nh, dh):
    qkv_refs, z_ref = refs[:ng], refs[ng]
    cache_refs, y_ref = refs[ng + 1:3 * ng + 1], refs[3 * ng + 1]
    bt = z_ref.shape[0]
    scale = dh ** -0.5
    pad = jnp.zeros((nh, dh), F32)
    rows16 = lambda t: jnp.concatenate([t, pad], axis=0)
    cases = [(i, g) for i in range(bt) for g in range(ng)]
    flat = lambda ref, i: ref[i].reshape(ref.shape[1] * nh, dh).astype(BF16)
    qs = {(i, g): rows16(qkv_refs[g][i, 0]) for i, g in cases}
    scores = {(i, g): _dot_nt(qs[i, g].astype(BF16), flat(cache_refs[2 * g], i)) for i, g in cases}
    probs = {}
    for i, g in cases:
        q, kn = qs[i, g], rows16(qkv_refs[g][i, 1])
        span = cache_refs[2 * g].shape[1]
        head = lax.broadcasted_iota(jnp.int32, (2 * nh, span * nh), 0)
        key_head = lax.broadcasted_iota(jnp.int32, (2 * nh, span * nh), 1) & (nh - 1)
        s = jnp.where(key_head == head, scores[i, g] * scale, -jnp.inf)
        s0 = jnp.sum(q * kn, axis=1, keepdims=True) * scale
        mx = jnp.maximum(jnp.max(s, axis=1, keepdims=True), s0)
        pr = jnp.exp(s - mx)
        p0 = jnp.exp(s0 - mx)
        den = jnp.sum(pr, axis=1, keepdims=True) + p0
        inv = 1.0 / den
        probs[i, g] = ((pr * inv).astype(BF16), p0 * inv, mx + jnp.log(den))
    outs = {(i, g): _dot(probs[i, g][0], flat(cache_refs[2 * g + 1], i)) for i, g in cases}
    for i in range(bt):
        lses = [probs[i, g][2] for g in range(ng)]
        mxl = functools.reduce(jnp.maximum, lses)
        es = [jnp.exp(l - mxl) for l in lses]
        inv = 1.0 / functools.reduce(jnp.add, es)
        terms = [(es[g] * inv) * (outs[i, g] + probs[i, g][1] * rows16(qkv_refs[g][i, 2])) for g in range(ng)]
        z = z_ref[i]
        y_ref[i] = functools.reduce(jnp.add, terms)[0:nh] * (z * _sigmoid(z))


def _attn_sample(qkvs, z, caches, layer, *, nh, dh, bt):
    DB = z.shape[0]
    ng = len(qkvs)
    assert nh & (nh - 1) == 0
    views, specs = [], []
    for g, (win, dil) in enumerate(A_GROUPS):
        span = win // dil
        for c in caches[2 * g:2 * g + 2]:
            assert c.shape[2] == win, "decode path expects a full window of cached rows"
            views.append(c.reshape(c.shape[0], DB, span, dil, nh, dh))
            specs.append(pl.BlockSpec((None, bt, span, None, nh, dh), lambda s: (layer, s, 0, 0, 0, 0)))
    kern = functools.partial(_attn_sample_kernel, ng=ng, nh=nh, dh=dh)
    return pl.pallas_call(
        kern,
        out_shape=jax.ShapeDtypeStruct((DB, nh, dh), F32),
        grid=(DB // bt,),
        in_specs=[pl.BlockSpec((bt, 3, nh, dh), lambda s: (s, 0, 0, 0))] * ng
        + [pl.BlockSpec((bt, nh, dh), lambda s: (s, 0, 0))] + specs,
        out_specs=pl.BlockSpec((bt, nh, dh), lambda s: (s, 0, 0)),
        compiler_params=_params(1),
        name="attn_sample",
    )(*qkvs, z, *views)


def _rope_tables(pos, dh):
    half = dh // 2
    inv = ROPE_THETA ** (-jnp.arange(half, dtype=F32) / half)
    ang = pos.astype(F32)[:, None] * inv[None, :]
    cos, sin = jnp.cos(ang), jnp.sin(ang)
    return jnp.concatenate([cos, cos], axis=1), jnp.concatenate([-sin, sin], axis=1)


def _row_tile(t, cap):
    tm = min(t, cap)
    while t % tm:
        tm //= 2
    return tm


def kernel(x_prompt, x_sample, state_C, state_n, state_m, cache_k1, cache_v1, cache_k2, cache_v2, cache_k3, cache_v3, ln_g, m_w_in, m_b_i, m_b_f, m_g_h, m_w_out, a_w_in, a_g_q, a_g_k, a_w_out):
    B, S, D = x_prompt.shape
    DB, DS, _ = x_sample.shape
    assert DS == 1
    depth = ln_g.shape[0]
    H = m_b_i.shape[1]
    INNER = m_w_out.shape[1]
    QK = (m_w_in.shape[2] - 3 * INNER - 2 * H) // 2
    DQK, DV = QK // H, INNER // H
    ng, dh = a_g_q.shape[1], a_g_q.shape[2]
    GW = a_w_out.shape[1]
    nh = GW // dh
    assert ng == len(A_GROUPS) and dh == LANES
    caches = (cache_k1, cache_v1, cache_k2, cache_v2, cache_k3, cache_v3)

    xp = x_prompt.reshape(B * S, D)
    xs = x_sample.reshape(DB, D)
    tm_p = _row_tile(S, 1024)
    tm_s = DB
    cos_p, sin_p = _rope_tables(jnp.arange(S), dh)
    cos_s, sin_s = _rope_tables(jnp.full((DB,), PAST_LEN), dh)
    ones = jnp.ones((1, (ng + 1) * GW), F32)
    dils = tuple(dil for _, dil in A_GROUPS)

    mp, ms, ap, as_ = [], [], [], []
    c_s = None
    for i in range(depth):
        j = i // 2
        g = ln_g[i].reshape(1, D)
        if i % 2 == 0:
            w_main = m_w_in[j][:, :2 * QK + 3 * INNER].astype(BF16)
            w_gate = jnp.pad(m_w_in[j][:, 2 * QK + 3 * INNER:], ((0, 0), (0, LANES - 2 * H))).astype(BF16)
            w_out = m_w_out[j].astype(BF16)
            bias = jnp.pad(jnp.concatenate([m_b_i[j], m_b_f[j]]), (0, LANES - 2 * H)).reshape(1, LANES)
            gh = m_g_h[j].reshape(1, INNER)

            p, gates = _proj_mlstm(xp, g, w_main, w_gate, tm=tm_p, tn=1024, out_dtype=BF16)
            y, c_p, n_p, m_p = _mlstm_prompt(p, gates, bias, gh, B=B, S=S, H=H, DQK=DQK, DV=DV)
            xp = _outproj(y, w_out, xp, tm=_row_tile(B * S, 512))
            mp.append((c_p, n_p.reshape(B, H, DQK), m_p[:, 0, :H]))

            p, gates = _proj_mlstm(xs, g, w_main, w_gate, tm=tm_s, tn=1024, out_dtype=F32)
            m_in = jnp.pad(state_m[j], ((0, 0), (0, LANES - H)))
            y, c_s, n_s, m_s = _mlstm_sample(p, gates, bias, gh, state_C, j, c_s, state_n[j].reshape(DB, QK), m_in,
                                             H=H, DQK=DQK, DV=DV, bt=8)
            xs = _outproj(y, w_out, xs, tm=tm_s)
            ms.append((n_s.reshape(DB, H, DQK), m_s[:, :H]))
        else:
            w_in = a_w_in[j].astype(BF16)
            w_out = a_w_out[j].astype(BF16)
            gain = jnp.concatenate([jnp.tile(a_g_q[j], (1, nh)).reshape(1, ng * GW),
                                    jnp.tile(a_g_k[j], (1, nh)).reshape(1, ng * GW), ones], axis=1)

            qkvs, z = _proj_attn(xp, g, w_in, gain, cos_p, sin_p, B=B, tm=tm_p, tn=GW, ng=ng, dh=dh, dils=dils,
                                 out_dtype=BF16)
            os_, ls_ = zip(*[_attn_prompt(qkvs[gi], nh=nh, dh=dh, win=win) for gi, (win, _) in enumerate(A_GROUPS)])
            xp = _outproj_comb(os_, ls_, z, w_out, xp, B=B, tm=_row_tile(S, 512), nh=nh, dh=dh, dils=dils)
            rows = []
            for gi, (win, dil) in enumerate(A_GROUPS):
                n_keep = min(win, S) // dil
                tail = qkvs[gi][:, :, S // dil - n_keep:, GW:].astype(F32)
                tail = tail.transpose(0, 2, 1, 3).reshape(B, n_keep * dil, 2, nh, dh)
                rows += [tail[:, :, 0], tail[:, :, 1]]
            ap.append(rows)

            qkvs, z = _proj_attn(xs, g, w_in, gain, cos_s, sin_s, B=1, tm=tm_s, tn=GW, ng=ng, dh=dh, dils=(1,) * ng,
                                 out_dtype=F32)
            qkvs = [t.reshape(DB, 3, nh, dh) for t in qkvs]
            y = _attn_sample(qkvs, z.reshape(DB, nh, dh), caches, j, nh=nh, dh=dh, bt=4)
            xs = _outproj(y.reshape(DB, GW), w_out, xs, tm=tm_s)
            as_.append([t[:, c].reshape(DB, 1, nh, dh) for t in qkvs for c in (1, 2)])

    stack = lambda items, k: jnp.stack([it[k] for it in items])
    return ((xp.reshape(B, S, D), xs.reshape(DB, DS, D), stack(mp, 0), stack(mp, 1), stack(mp, 2))
            + tuple(stack(ap, k) for k in range(2 * ng))
            + (c_s, stack(ms, 0), stack(ms, 1))
            + tuple(stack(as_, k) for k in range(2 * ng)))
```

```python
import functools

import jax
import jax.numpy as jnp
from jax import lax
from jax.experimental import pallas as pl
from jax.experimental.pallas import tpu as pltpu

F32 = jnp.float32
BF16 = jnp.bfloat16
EPS = 1e-6
LANES = 128
VMEM_LIMIT = 56 * 1024 * 1024

A_GROUPS = ((128, 1), (512, 4), (2048, 16))
PAST_LEN = 2048
ROPE_THETA = 10000.0
A_BLK = 128
A_STEP_BLOCKS = 4
M_BLOCK = 256
PROJ_UNIT_ROWS = 512


def _params(n_axes):
    return pltpu.CompilerParams(dimension_semantics=("arbitrary",) * n_axes, vmem_limit_bytes=VMEM_LIMIT)


def _sigmoid(x):
    return 0.5 * jnp.tanh(0.5 * x) + 0.5


def _log_sigmoid(x):
    return jnp.minimum(x, 0.0) - jnp.log1p(jnp.exp(-jnp.abs(x)))


def _rms_rows(x, g):
    ms = jnp.mean(x * x, axis=-1, keepdims=True)
    return x * lax.rsqrt(ms + EPS) * g


def _split3(x):
    x1 = x.astype(BF16)
    r1 = x - x1.astype(F32)
    x2 = r1.astype(BF16)
    x3 = (r1 - x2.astype(F32)).astype(BF16)
    return x1, x2, x3


def _dot(a, b):
    return jnp.dot(a, b, preferred_element_type=F32)


def _dot_nt(a, b):
    return lax.dot_general(a, b, (((1,), (1,)), ((), ())), preferred_element_type=F32)


def _dot_tn(a, b):
    return lax.dot_general(a, b, (((0,), (0,)), ((), ())), preferred_element_type=F32)


def _proj_mlstm_kernel(x_ref, g_ref, w_ref, wg_ref, o_ref, gate_ref, xn_ref):
    @pl.when(pl.program_id(1) == 0)
    def _():
        xn = _rms_rows(x_ref[...], g_ref[...]).astype(BF16)
        xn_ref[...] = xn
        gate_ref[...] = _dot(xn, wg_ref[...])

    o_ref[...] = _dot(xn_ref[...], w_ref[...]).astype(o_ref.dtype)


def _proj_mlstm(x, g, w, wg, *, tm, tn, out_dtype):
    T, D = x.shape
    P = w.shape[1] // tn * tn
    return pl.pallas_call(
        _proj_mlstm_kernel,
        out_shape=(jax.ShapeDtypeStruct((T, P), out_dtype), jax.ShapeDtypeStruct((T, LANES), F32)),
        grid=(T // tm, P // tn),
        in_specs=[
            pl.BlockSpec((tm, D), lambda i, j: (i, 0)),
            pl.BlockSpec((1, D), lambda i, j: (0, 0)),
            pl.BlockSpec((D, tn), lambda i, j: (0, j)),
            pl.BlockSpec((D, LANES), lambda i, j: (0, 0)),
        ],
        out_specs=(
            pl.BlockSpec((tm, tn), lambda i, j: (i, j)),
            pl.BlockSpec((tm, LANES), lambda i, j: (i, 0)),
        ),
        scratch_shapes=[pltpu.VMEM((tm, D), BF16)],
        compiler_params=_params(2),
        name="proj_mlstm",
    )(x, g, w, wg)


def _proj_attn_kernel(x_ref, g_ref, w_ref, gain_ref, *refs, ng, dh, dils):
    cos_refs, sin_refs, refs = refs[:ng], refs[ng:2 * ng], refs[2 * ng:]
    outs, z_ref, xn_ref, acc_ref = refs[:ng], refs[ng], refs[ng + 1], refs[ng + 2]
    j = pl.program_id(1)
    n_slabs, tm, _ = acc_ref.shape

    @pl.when(j == 0)
    def _():
        xn_ref[...] = _rms_rows(x_ref[...], g_ref[...]).astype(BF16)

    @pl.when(j == 3 * ng)
    def _():
        z_ref[...] = _dot(xn_ref[...], w_ref[...]).astype(z_ref.dtype)

    def project(p):
        acc = _dot(xn_ref[...], w_ref[:, 2 * p * dh:2 * (p + 1) * dh])
        acc_ref[2 * p] = acc[:, :dh]
        acc_ref[2 * p + 1] = acc[:, dh:]

    def tile(g, dil, rot):
        per_res = tm // dil
        unit_rows = min(tm, PROJ_UNIT_ROWS)
        units = []
        for u in range(tm // unit_rows):
            lo = u * unit_rows
            if per_res >= unit_rows:
                units.append([(lo // per_res, lo % per_res, unit_rows)])
            else:
                units.append([(lo // per_res + k, 0, per_res) for k in range(unit_rows // per_res)])

        def gather(ref, seg, *lead):
            r, i0, n = seg
            rows = pl.ds(i0, n) if dil == 1 else pl.ds(r + dil * i0, n, stride=dil)
            return ref[(*lead, rows, slice(None))]

        if rot:
            kk = lax.broadcasted_iota(jnp.int32, (2 * dh, 2 * dh), 0)
            cc = lax.broadcasted_iota(jnp.int32, (2 * dh, 2 * dh), 1)
            head_sum = jnp.where(_head_of(kk, dh) == _head_of(cc, dh), 1.0, 0.0).astype(BF16)
            half_swap = jnp.where(kk == (cc ^ (dh // 2)), 1.0, 0.0).astype(BF16)
        def finish(p):
            vals = [jnp.concatenate(
                [jnp.concatenate([gather(acc_ref, sg, 2 * p + t) for sg in segs], axis=0) for t in range(2)], axis=1)
                for segs in units]
            if rot:
                tabs = []
                for u in range(len(units)):
                    cos = cos_refs[g][u * unit_rows:(u + 1) * unit_rows, :]
                    sin = sin_refs[g][u * unit_rows:(u + 1) * unit_rows, :]
                    tabs.append((jnp.concatenate([cos, cos], axis=1), jnp.concatenate([sin, sin], axis=1)))
                sums = [_dot((a * a).astype(BF16), head_sum) for a in vals]
                gain = gain_ref[:, 2 * p * dh:2 * (p + 1) * dh]
                vals = [a * lax.rsqrt(ss * (1.0 / dh) + EPS) * gain for a, ss in zip(vals, sums)]
                his = [a.astype(BF16) for a in vals]
                los = [(a - hi.astype(F32)).astype(BF16) for a, hi in zip(vals, his)]
                swapped = [_dot(hi, half_swap) + _dot(lo, half_swap) for hi, lo in zip(his, los)]
                vals = [a * cos + sw * sin for a, sw, (cos, sin) in zip(vals, swapped, tabs)]
            for segs, a in zip(units, vals):
                a = a.astype(outs[g].dtype)
                row = 0
                for r, i0, n in segs:
                    outs[g][r, i0:i0 + n, 2 * p * dh:2 * (p + 1) * dh] = a[row:row + n]
                    row += n

        n_pairs = n_slabs // 2
        project(0)
        for p in range(n_pairs):
            if p + 1 < n_pairs:
                project(p + 1)
            finish(p)

    for g, dil in enumerate(dils):
        @pl.when((j < 2 * ng) & (lax.rem(j, ng) == g))
        def _(g=g, dil=dil):
            tile(g, dil, True)

        @pl.when((j >= 2 * ng) & (j < 3 * ng) & (lax.rem(j, ng) == g))
        def _(g=g, dil=dil):
            tile(g, dil, False)


def _proj_attn(x, g, w, gain, cos, sin, *, B, tm, tn, ng, dh, dils, out_dtype):
    T, D = x.shape
    S = T // B
    tiles = S // tm
    pos_blocks = cos.shape[0] // tm
    assert w.shape[1] == (3 * ng + 1) * tn and all(tm % d == 0 for d in dils)
    kern = functools.partial(_proj_attn_kernel, ng=ng, dh=dh, dils=dils)

    def tile_major(tab, dil):
        return tab.reshape(pos_blocks, tm // dil, dil, dh).transpose(0, 2, 1, 3).reshape(pos_blocks * tm, dh)

    tables = [tile_major(cos, dil) for dil in dils] + [tile_major(sin, dil) for dil in dils]
    table_spec = pl.BlockSpec((tm, dh), lambda i, j: (i % pos_blocks, 0))

    def group_spec(gi, dil):
        return pl.BlockSpec((None, dil, tm // dil, tn),
                            lambda i, j: (i // tiles, 0, i % tiles, jnp.clip(jnp.maximum(j - gi, 0) // ng, 0, 2)))

    *qkvs, z = pl.pallas_call(
        kern,
        out_shape=tuple(jax.ShapeDtypeStruct((B, dil, S // dil, 3 * tn), out_dtype) for dil in dils)
        + (jax.ShapeDtypeStruct((T, tn), out_dtype),),
        grid=(T // tm, 3 * ng + 1),
        in_specs=[
            pl.BlockSpec((tm, D), lambda i, j: (i, 0)),
            pl.BlockSpec((1, D), lambda i, j: (0, 0)),
            pl.BlockSpec((D, tn), lambda i, j: (0, j)),
            pl.BlockSpec((1, tn), lambda i, j: (0, j)),
        ] + [table_spec] * (2 * ng),
        out_specs=tuple(group_spec(gi, dil) for gi, dil in enumerate(dils))
        + (pl.BlockSpec((tm, tn), lambda i, j: (i, 0)),),
        scratch_shapes=[pltpu.VMEM((tm, D), BF16), pltpu.VMEM((tn // dh, tm, dh), F32)],
        compiler_params=_params(2),
        name="proj_attn",
    )(x, g, w, gain, *tables)
    return qkvs, z


def _outproj_kernel(y_ref, w_ref, r_ref, o_ref):
    o_ref[...] = r_ref[...] + _dot(y_ref[...].astype(BF16), w_ref[...])


def _outproj(y, w, resid, *, tm):
    T, E = y.shape
    D = w.shape[1]
    return pl.pallas_call(
        _outproj_kernel,
        out_shape=jax.ShapeDtypeStruct((T, D), F32),
        grid=(T // tm,),
        in_specs=[
            pl.BlockSpec((tm, E), lambda i: (i, 0)),
            pl.BlockSpec((E, D), lambda i: (0, 0)),
            pl.BlockSpec((tm, D), lambda i: (i, 0)),
        ],
        out_specs=pl.BlockSpec((tm, D), lambda i: (i, 0)),
        compiler_params=_params(1),
        name="outproj",
    )(y, w, resid)


def _head_of(col, dh):
    return lax.shift_right_logical(col, dh.bit_length() - 1)


def _head_expander(nh, dh):
    r = lax.broadcasted_iota(jnp.int32, (LANES, nh * dh), 0)
    c = lax.broadcasted_iota(jnp.int32, (LANES, nh * dh), 1)
    return jnp.where(_head_of(c, dh) == r, 1.0, 0.0).astype(BF16)


def _outproj_comb_kernel(*refs, ng, nh, dh, dils):
    o_refs, l_refs = refs[:ng], refs[ng:2 * ng]
    z_ref, w_ref, r_ref, out_ref, ot_ref, lt_ref = refs[2 * ng:]
    tm = out_ref.shape[0]
    for g, dil in enumerate(dils):
        for r in range(dil):
            rows = slice(None) if dil == 1 else pl.ds(r, tm // dil, stride=dil)
            lt_ref[g, rows, :] = l_refs[g][r]
            for s in range(nh):
                ot_ref[g, s, rows, :] = o_refs[g][r, :, s * dh:(s + 1) * dh].astype(F32)
    lses = [lt_ref[g] for g in range(ng)]
    mx = functools.reduce(jnp.maximum, lses)
    es = [jnp.exp(l - mx) for l in lses]
    inv = 1.0 / functools.reduce(jnp.add, es)
    expand = _head_expander(nh, dh)
    y = None
    for g in range(ng):
        w_hi, w_lo, _ = _split3(es[g] * inv)
        wide = _dot(w_hi, expand) + _dot(w_lo, expand)
        term = wide * jnp.concatenate([ot_ref[g, s] for s in range(nh)], axis=1)
        y = term if y is None else y + term
    z = z_ref[...].astype(F32)
    y = y * (z * _sigmoid(z))
    out_ref[...] = r_ref[...] + _dot(y.astype(BF16), w_ref[...])


def _outproj_comb(os_, ls_, z, w, resid, *, B, tm, nh, dh, dils):
    T, D = resid.shape
    E = w.shape[0]
    ng = len(dils)
    tiles = T // B // tm
    kern = functools.partial(_outproj_comb_kernel, ng=ng, nh=nh, dh=dh, dils=dils)
    res_spec = lambda dil, width: pl.BlockSpec((None, dil, tm // dil, width), lambda i: (i // tiles, 0, i % tiles, 0))
    return pl.pallas_call(
        kern,
        out_shape=jax.ShapeDtypeStruct((T, D), F32),
        grid=(T // tm,),
        in_specs=[res_spec(dil, E) for dil in dils] + [res_spec(dil, LANES) for dil in dils]
        + [pl.BlockSpec((tm, E), lambda i: (i, 0)),
           pl.BlockSpec((E, D), lambda i: (0, 0)),
           pl.BlockSpec((tm, D), lambda i: (i, 0))],
        out_specs=pl.BlockSpec((tm, D), lambda i: (i, 0)),
        scratch_shapes=[pltpu.VMEM((ng, nh, tm, dh), F32), pltpu.VMEM((ng, tm, LANES), F32)],
        compiler_params=_params(1),
        name="outproj_comb",
    )(*os_, *ls_, z, w, resid)


def _mlstm_prompt_kernel(q_ref, k_ref, v_ref, o_ref, z_ref, gt_ref, bias_ref, gh_ref, wo_ref, x_ref,
                         xo_ref, c_ref, n_ref, m_ref, *, H, DQK, DV):
    L = q_ref.shape[0]
    scale = DQK ** -0.5

    @pl.when(pl.program_id(1) == 0)
    def _():
        c_ref[...] = jnp.zeros_like(c_ref)
        n_ref[...] = jnp.zeros_like(n_ref)
        m_ref[...] = jnp.zeros_like(m_ref)

    gates = gt_ref[...] + bias_ref[...]
    lane = lax.broadcasted_iota(jnp.int32, gates.shape, 1)
    x = jnp.where(lane < H, gates, _log_sigmoid(gates))
    row = lax.broadcasted_iota(jnp.int32, (L, L), 0)
    col = lax.broadcasted_iota(jnp.int32, (L, L), 1)
    causal = col <= row
    tri = jnp.where(causal, 1.0, 0.0).astype(BF16)
    x1, x2, x3 = _split3(x)
    cum = _dot(tri, x1) + _dot(tri, x2) + _dot(tri, x3)
    xt = x.T
    cumt = cum.T

    m_all = m_ref[0]
    lane1 = lax.broadcasted_iota(jnp.int32, m_all.shape, 1)
    qss = [slice(h * DQK, (h + 1) * DQK) for h in range(H)]
    vss = [slice(h * DV, (h + 1) * DV) for h in range(H)]
    st = [dict() for _ in range(H)]
    gh4 = 0.25 * gh_ref[...]

    def stage_a(h):
        st[h]["qk"] = _dot_nt(q_ref[:, qss[h]], k_ref[:, qss[h]])
        st[h]["qc"] = _dot(q_ref[:, qss[h]], c_ref[0, h].astype(BF16))

    def stage_b(h):
        qs = qss[h]
        ig_row = xt[h:h + 1, :]
        ig_col = x[:, h:h + 1]
        b_row = cumt[H + h:H + h + 1, :]
        b_col = cum[:, H + h:H + h + 1]
        g_tot = cumt[H + h:H + h + 1, L - 1:L]
        m_prev = m_all[:, h:h + 1]

        dlog = jnp.where(causal, b_col - b_row + ig_row, -jnp.inf)
        inter = b_col + m_prev
        m_t = jnp.maximum(inter, jnp.max(dlog, axis=1, keepdims=True))
        w_inter = jnp.exp(inter - m_t)
        w_intra = jnp.exp(dlog - m_t) * (st[h].pop("qk") * scale)
        n_prev = n_ref[0, :, qs]
        den = (w_inter * jnp.sum(q_ref[:, qs].astype(F32) * n_prev, axis=1, keepdims=True)
               + jnp.sum(w_intra, axis=1, keepdims=True))
        inv = 1.0 / jnp.maximum(jnp.abs(den), jnp.exp(-m_t))

        a_col = ig_col + g_tot - b_col
        m_new = jnp.maximum(g_tot + m_prev, jnp.max(a_col, axis=0, keepdims=True))
        decay = jnp.exp(g_tot + m_prev - m_new)
        kw = k_ref[:, qs].astype(F32) * (jnp.exp(a_col - m_new) * scale)
        n_ref[0, :, qs] = decay * n_prev + jnp.sum(kw, axis=0, keepdims=True)
        st[h].update(w_inter=w_inter, w_intra=w_intra.astype(BF16), inv=inv, decay=decay, kw=kw.astype(BF16),
                     m_new=m_new)

    def stage_c(h):
        st[h]["wv"] = _dot(st[h].pop("w_intra"), v_ref[:, vss[h]])
        st[h]["kv"] = _dot_tn(st[h].pop("kw"), v_ref[:, vss[h]])

    def stage_d(h):
        vs = vss[h]
        hid = (st[h]["w_inter"] * st[h]["qc"] + st[h]["wv"]) * st[h]["inv"]
        hid = hid * lax.rsqrt(jnp.mean(hid * hid, axis=1, keepdims=True) + EPS)
        og = o_ref[:, vs].astype(F32)
        zg = z_ref[:, vs].astype(F32)
        gate = (jnp.tanh(0.5 * og) + 1.0) * (jnp.tanh(0.5 * zg) + 1.0) * zg
        y = (hid * gh4[:, vs] * gate).astype(BF16)
        st[0]["out"] = st[0].get("out", x_ref[...]) + _dot(y, wo_ref[vs, :])
        c_ref[0, h] = st[h]["decay"] * c_ref[0, h] + st[h]["kv"]

    order = [("a", 0)]
    for h in range(H):
        if h + 1 < H:
            order.append(("a", h + 1))
        order += [("b", h), ("c", h)]
        if h > 0:
            order.append(("d", h - 1))
    order.append(("d", H - 1))
    stages = dict(a=stage_a, b=stage_b, c=stage_c, d=stage_d)
    for name, h in order:
        stages[name](h)
    xo_ref[...] = st[0]["out"]
    m_next = m_all
    for h in range(H):
        m_next = jnp.where(lane1 == h, st[h]["m_new"], m_next)
    m_ref[0] = m_next


def _mlstm_prompt(p, gates, bias, gh, w_out, resid, *, B, S, H, DQK, DV):
    L = min(M_BLOCK, S)
    nc = S // L
    QK, INNER = H * DQK, H * DV
    D = w_out.shape[1]
    kern = functools.partial(_mlstm_prompt_kernel, H=H, DQK=DQK, DV=DV)
    row = lambda b, c: b * nc + c
    v_blk = 2 * QK // INNER
    return pl.pallas_call(
        kern,
        out_shape=(jax.ShapeDtypeStruct((B * S, D), F32),
                   jax.ShapeDtypeStruct((B, H, DQK, DV), F32),
                   jax.ShapeDtypeStruct((B, 1, QK), F32),
                   jax.ShapeDtypeStruct((B, 1, LANES), F32)),
        grid=(B, nc),
        in_specs=[
            pl.BlockSpec((L, QK), lambda b, c: (row(b, c), 0)),
            pl.BlockSpec((L, QK), lambda b, c: (row(b, c), 1)),
            pl.BlockSpec((L, INNER), lambda b, c: (row(b, c), v_blk)),
            pl.BlockSpec((L, INNER), lambda b, c: (row(b, c), v_blk + 1)),
            pl.BlockSpec((L, INNER), lambda b, c: (row(b, c), v_blk + 2)),
            pl.BlockSpec((L, LANES), lambda b, c: (row(b, c), 0)),
            pl.BlockSpec((1, LANES), lambda b, c: (0, 0)),
            pl.BlockSpec((1, INNER), lambda b, c: (0, 0)),
            pl.BlockSpec((INNER, D), lambda b, c: (0, 0)),
            pl.BlockSpec((L, D), lambda b, c: (row(b, c), 0)),
        ],
        out_specs=(
            pl.BlockSpec((L, D), lambda b, c: (row(b, c), 0)),
            pl.BlockSpec((1, H, DQK, DV), lambda b, c: (b, 0, 0, 0)),
            pl.BlockSpec((1, 1, QK), lambda b, c: (b, 0, 0)),
            pl.BlockSpec((1, 1, LANES), lambda b, c: (b, 0, 0)),
        ),
        compiler_params=_params(2),
        name="mlstm_prompt",
    )(p, p, p, p, p, gates, bias, gh, w_out, resid)


SC_ROWS = 16


def _mlstm_sample_kernel(p_ref, gt_ref, bias_ref, gh_ref, n_ref, m_ref, v_ref, c_ref, *refs, H, DQK, DV, bt, layer,
                         first):
    (y_ref, c_out_ref, n_out_ref, m_out_ref, qkt_ref, sct_ref, hq_ref, a_ref, bc_ref) = refs[-9:]
    if first:
        for other in range(c_out_ref.shape[0]):
            if other != layer:
                c_out_ref[other] = jnp.zeros(c_out_ref.shape[1:], F32)
        c_out_ref = c_out_ref.at[layer]
    step = pl.program_id(0)
    head = pl.program_id(1)
    DB = p_ref.shape[0]
    QK, INNER = H * DQK, H * DV
    scale = DQK ** -0.5

    @pl.when((step == 0) & (head == 0))
    def _():
        gates = gt_ref[...] + bias_ref[...]
        lane = lax.broadcasted_iota(jnp.int32, gates.shape, 1)
        ig = jnp.where(lane < H, gates, 0.0)
        lf = jnp.where(lane < H, pltpu.roll(_log_sigmoid(gates), LANES - H, axis=1), 0.0)
        m_prev = m_ref[...]
        m_t = jnp.maximum(lf + m_prev, ig)
        w_inter = jnp.exp(lf + m_prev - m_t)
        wa = jnp.exp(ig - m_t)
        q = p_ref[:, 0:QK].astype(F32)
        k = p_ref[:, QK:2 * QK].astype(F32)
        n_prev = n_ref[...]
        qk = jnp.zeros_like(gates)
        qn = jnp.zeros_like(gates)
        for h in range(H):
            qs = slice(h * DQK, (h + 1) * DQK)
            qk = jnp.where(lane == h, jnp.sum(q[:, qs] * k[:, qs], axis=1, keepdims=True) * scale, qk)
            qn = jnp.where(lane == h, jnp.sum(q[:, qs] * n_prev[:, qs], axis=1, keepdims=True), qn)
            n_out_ref[:, qs] = w_inter[:, h:h + 1] * n_prev[:, qs] + (wa[:, h:h + 1] * scale) * k[:, qs]
        w_intra = wa * qk
        den = w_inter * qn + w_intra
        inv = 1.0 / jnp.maximum(jnp.abs(den), jnp.exp(-m_t))
        a_ref[...] = w_inter * inv
        bc_ref[...] = w_intra * inv
        m_out_ref[...] = jnp.where(lane < H, m_t, 0.0)
        qkt_ref[...] = p_ref[:, 0:2 * QK].astype(F32).T.astype(BF16)
        sc = jnp.where(lane < H, w_inter, pltpu.roll(wa * scale, H, axis=1))
        sct_ref[...] = jnp.where(lane < 2 * H, sc, 0.0).T[0:SC_ROWS, :]

    r = lax.broadcasted_iota(jnp.int32, (DB, LANES), 0)
    srow = lax.broadcasted_iota(jnp.int32, (SC_ROWS, LANES), 0)
    trow = lax.broadcasted_iota(jnp.int32, (bt, DV), 0)
    qt = qkt_ref[pl.ds(pl.multiple_of(head * DQK, DQK), DQK), :]
    kt = qkt_ref[pl.ds(pl.multiple_of(QK + head * DQK, DQK), DQK), :]
    s1, s2, s3 = _split3(sct_ref[...])
    tile = jnp.zeros((bt, DV), F32)
    for i in range(bt):
        onehot = jnp.where(r == step * bt + i, 1.0, 0.0).astype(BF16)
        qcol = _dot(qt, onehot)
        kcol = _dot(kt, onehot)
        scal = _dot(s1, onehot) + _dot(s2, onehot) + _dot(s3, onehot)
        decay = jnp.sum(jnp.where(srow == head, scal, 0.0), axis=0, keepdims=True)
        wsc = jnp.sum(jnp.where(srow == H + head, scal, 0.0), axis=0, keepdims=True)
        hrow = []
        for t in range(DV // LANES):
            cs = slice(t * LANES, (t + 1) * LANES)
            c_prev = c_ref[i, 0, :, cs]
            hrow.append(jnp.sum(qcol * c_prev, axis=0, keepdims=True))
            c_out_ref[i, 0, :, cs] = decay * c_prev + kcol * (wsc * v_ref[i:i + 1, cs])
        tile = jnp.where(trow == i, jnp.concatenate(hrow, axis=1), tile)
    hq_ref[head, pl.ds(pl.multiple_of(step * bt, bt), bt), :] = tile

    @pl.when((step == pl.num_programs(0) - 1) & (head == H - 1))
    def _():
        for h in range(H):
            vs = slice(h * DV, (h + 1) * DV)
            v = p_ref[:, 2 * QK + h * DV:2 * QK + (h + 1) * DV].astype(F32)
            hid = a_ref[:, h:h + 1] * hq_ref[h] + bc_ref[:, h:h + 1] * v
            hid = hid * lax.rsqrt(jnp.mean(hid * hid, axis=1, keepdims=True) + EPS)
            og = p_ref[:, 2 * QK + INNER + h * DV:2 * QK + INNER + (h + 1) * DV].astype(F32)
            zg = p_ref[:, 2 * QK + 2 * INNER + h * DV:2 * QK + 2 * INNER + (h + 1) * DV].astype(F32)
            y_ref[:, vs] = (hid * gh_ref[:, vs] * _sigmoid(og) * (zg * _sigmoid(zg))).astype(BF16)


def _mlstm_sample(p, gates, bias, gh, c_all, layer, c_out_prev, n_in, m_in, *, H, DQK, DV, bt):
    DB = p.shape[0]
    QK, INNER = H * DQK, H * DV
    assert 2 * H <= SC_ROWS and DB % bt == 0
    first = c_out_prev is None
    kern = functools.partial(_mlstm_sample_kernel, H=H, DQK=DQK, DV=DV, bt=bt, layer=layer, first=first)
    full = lambda shape: pl.BlockSpec(shape, lambda s, h: (0,) * len(shape))
    c_spec = pl.BlockSpec((None, bt, 1, DQK, DV), lambda s, h: (layer, s, h, 0, 0))
    c_out_spec = pl.BlockSpec((c_all.shape[0], bt, 1, DQK, DV), lambda s, h: (0, s, h, 0, 0)) if first else c_spec
    v_heads = p[:, 2 * QK:2 * QK + INNER].reshape(DB, H, DV).transpose(1, 0, 2)
    args = [p, gates, bias, gh, n_in, m_in, v_heads, c_all]
    in_specs = [full(p.shape), full((DB, LANES)), full((1, LANES)), full((1, INNER)),
                full((DB, QK)), full((DB, LANES)),
                pl.BlockSpec((None, bt, DV), lambda s, h: (h, s, 0)), c_spec]
    aliases = {}
    if c_out_prev is not None:
        aliases = {len(args): 1}
        args.append(c_out_prev)
        in_specs.append(pl.BlockSpec(memory_space=pl.ANY))
    return pl.pallas_call(
        kern,
        out_shape=(jax.ShapeDtypeStruct((DB, INNER), BF16),
                   jax.ShapeDtypeStruct(c_all.shape, F32),
                   jax.ShapeDtypeStruct((DB, QK), F32),
                   jax.ShapeDtypeStruct((DB, LANES), F32)),
        grid=(DB // bt, H),
        in_specs=in_specs,
        out_specs=(full((DB, INNER)), c_out_spec, full((DB, QK)), full((DB, LANES))),
        scratch_shapes=[pltpu.VMEM((2 * QK, DB), BF16), pltpu.VMEM((SC_ROWS, DB), F32),
                        pltpu.VMEM((H, DB, DV), F32), pltpu.VMEM((DB, LANES), F32), pltpu.VMEM((DB, LANES), F32)],
        input_output_aliases=aliases,
        compiler_params=_params(2),
        name="mlstm_sample",
    )(*args)


def _attn_prompt_kernel(q_ref, k_ref, v_ref, o_ref, lse_ref, kp_ref, vp_ref, *, nh, dh, span, blk):
    n = pl.program_id(2)
    nq = q_ref.shape[0] // blk
    scale = dh ** -0.5

    @pl.when(n == 0)
    def _():
        kp_ref[...] = jnp.zeros_like(kp_ref)
        vp_ref[...] = jnp.zeros_like(vp_ref)

    iq = lax.broadcasted_iota(jnp.int32, (blk, blk), 0)
    ik = lax.broadcasted_iota(jnp.int32, (blk, blk), 1)
    mask_cur = (ik <= iq) & (iq - ik <= span)
    mask_prev = blk + iq - ik <= span
    mask_first = blk + iq - ik <= jnp.where(n > 0, span, -1)
    lane = lax.broadcasted_iota(jnp.int32, (blk, LANES), 1)
    heads = [slice(h * dh, (h + 1) * dh) for h in range(nh)]

    def rows(t):
        return slice(t * blk, (t + 1) * blk)

    def prev_block(ref, carry_ref, t, hs):
        return carry_ref[:, hs] if t == 0 else ref[rows(t - 1), hs]

    for t0 in range(0, nq, 2):
        units = [(t, h) for t in range(t0, min(t0 + 2, nq)) for h in range(nh)]
        scores = {(t, h): (_dot_nt(q_ref[rows(t), heads[h]], k_ref[rows(t), heads[h]]),
                           _dot_nt(q_ref[rows(t), heads[h]], prev_block(k_ref, kp_ref, t, heads[h])))
                  for t, h in units}
        probs = {}
        lse = {t: jnp.zeros((blk, LANES), F32) for t, _ in units}
        for t, h in units:
            s_cur, s_prev = scores[t, h]
            s_cur = jnp.where(mask_cur, s_cur * scale, -jnp.inf)
            s_prev = jnp.where(mask_first if t == 0 else mask_prev, s_prev * scale, -jnp.inf)
            mx = jnp.maximum(jnp.max(s_cur, axis=1, keepdims=True), jnp.max(s_prev, axis=1, keepdims=True))
            p_cur = jnp.exp(s_cur - mx)
            p_prev = jnp.exp(s_prev - mx)
            den = jnp.sum(p_cur, axis=1, keepdims=True) + jnp.sum(p_prev, axis=1, keepdims=True)
            inv = 1.0 / den
            probs[t, h] = ((p_cur * inv).astype(BF16), (p_prev * inv).astype(BF16))
            lse[t] = jnp.where(lane == h, mx + jnp.log(den), lse[t])
        for t, h in units:
            p_cur, p_prev = probs[t, h]
            o_ref[rows(t), heads[h]] = (_dot(p_cur, v_ref[rows(t), heads[h]])
                                        + _dot(p_prev, prev_block(v_ref, vp_ref, t, heads[h]))).astype(BF16)
        for t in lse:
            lse_ref[rows(t), :] = lse[t]
    kp_ref[...] = k_ref[rows(nq - 1), :]
    vp_ref[...] = v_ref[rows(nq - 1), :]


def _attn_prompt(qkv, *, nh, dh, win):
    B, dil, Ls, _ = qkv.shape
    GW = nh * dh
    rows = min(A_STEP_BLOCKS * A_BLK, Ls)
    assert Ls % rows == 0
    kern = functools.partial(_attn_prompt_kernel, nh=nh, dh=dh, span=win // dil, blk=A_BLK)
    spec = lambda width, col: pl.BlockSpec((None, None, rows, width), lambda b, r, n: (b, r, n, col))
    return pl.pallas_call(
        kern,
        out_shape=(jax.ShapeDtypeStruct((B, dil, Ls, GW), BF16),
                   jax.ShapeDtypeStruct((B, dil, Ls, LANES), F32)),
        grid=(B, dil, Ls // rows),
        in_specs=[spec(GW, 0), spec(GW, 1), spec(GW, 2)],
        out_specs=(spec(GW, 0), spec(LANES, 0)),
        scratch_shapes=[pltpu.VMEM((A_BLK, GW), BF16), pltpu.VMEM((A_BLK, GW), BF16)],
        compiler_params=_params(3),
        name=f"attn_prompt_d{dil}",
    )(qkv, qkv, qkv)


def _attn_sample_kernel(*refs, ng, nh, dh):
    qkv_refs, z_ref = refs[:ng], refs[ng]
    cache_refs, y_ref = refs[ng + 1:3 * ng + 1], refs[3 * ng + 1]
    bt = z_ref.shape[0]
    scale = dh ** -0.5
    pad = jnp.zeros((nh, dh), F32)
    rows16 = lambda t: jnp.concatenate([t, pad], axis=0)
    cases = [(i, g) for i in range(bt) for g in range(ng)]
    flat = lambda ref, i: ref[i].reshape(ref.shape[1] * nh, dh).astype(BF16)
    qs = {(i, g): rows16(qkv_refs[g][i, 0]) for i, g in cases}
    scores = {(i, g): _dot_nt(qs[i, g].astype(BF16), flat(cache_refs[2 * g], i)) for i, g in cases}
    probs = {}
    for i, g in cases:
        q, kn = qs[i, g], rows16(qkv_refs[g][i, 1])
        span = cache_refs[2 * g].shape[1]
        head = lax.broadcasted_iota(jnp.int32, (2 * nh, span * nh), 0)
        key_head = lax.broadcasted_iota(jnp.int32, (2 * nh, span * nh), 1) & (nh - 1)
        s = jnp.where(key_head == head, scores[i, g] * scale, -jnp.inf)
        s0 = jnp.sum(q * kn, axis=1, keepdims=True) * scale
        mx = jnp.maximum(jnp.max(s, axis=1, keepdims=True), s0)
        pr = jnp.exp(s - mx)
        p0 = jnp.exp(s0 - mx)
        den = jnp.sum(pr, axis=1, keepdims=True) + p0
        inv = 1.0 / den
        probs[i, g] = ((pr * inv).astype(BF16), p0 * inv, mx + jnp.log(den))
    outs = {(i, g): _dot(probs[i, g][0], flat(cache_refs[2 * g + 1], i)) for i, g in cases}
    for i in range(bt):
        lses = [probs[i, g][2] for g in range(ng)]
        mxl = functools.reduce(jnp.maximum, lses)
        es = [jnp.exp(l - mxl) for l in lses]
        inv = 1.0 / functools.reduce(jnp.add, es)
        terms = [(es[g] * inv) * (outs[i, g] + probs[i, g][1] * rows16(qkv_refs[g][i, 2])) for g in range(ng)]
        z = z_ref[i]
        y_ref[i] = functools.reduce(jnp.add, terms)[0:nh] * (z * _sigmoid(z))


def _attn_sample(qkvs, z, caches, layer, *, nh, dh, bt):
    DB = z.shape[0]
    ng = len(qkvs)
    assert nh & (nh - 1) == 0
    views, specs = [], []
    for g, (win, dil) in enumerate(A_GROUPS):
        span = win // dil
        for c in caches[2 * g:2 * g + 2]:
            assert c.shape[2] == win, "decode path expects a full window of cached rows"
            views.append(c.reshape(c.shape[0], DB, span, dil, nh, dh))
            specs.append(pl.BlockSpec((None, bt, span, None, nh, dh), lambda s: (layer, s, 0, 0, 0, 0)))
    kern = functools.partial(_attn_sample_kernel, ng=ng, nh=nh, dh=dh)
    return pl.pallas_call(
        kern,
        out_shape=jax.ShapeDtypeStruct((DB, nh, dh), F32),
        grid=(DB // bt,),
        in_specs=[pl.BlockSpec((bt, 3, nh, dh), lambda s: (s, 0, 0, 0))] * ng
        + [pl.BlockSpec((bt, nh, dh), lambda s: (s, 0, 0))] + specs,
        out_specs=pl.BlockSpec((bt, nh, dh), lambda s: (s, 0, 0)),
        compiler_params=_params(1),
        name="attn_sample",
    )(*qkvs, z, *views)


def _rope_tables(pos, dh):
    half = dh // 2
    inv = ROPE_THETA ** (-jnp.arange(half, dtype=F32) / half)
    ang = pos.astype(F32)[:, None] * inv[None, :]
    cos, sin = jnp.cos(ang), jnp.sin(ang)
    return jnp.concatenate([cos, cos], axis=1), jnp.concatenate([-sin, sin], axis=1)


def _row_tile(t, cap):
    tm = min(t, cap)
    while t % tm:
        tm //= 2
    return tm


def kernel(x_prompt, x_sample, state_C, state_n, state_m, cache_k1, cache_v1, cache_k2, cache_v2, cache_k3, cache_v3, ln_g, m_w_in, m_b_i, m_b_f, m_g_h, m_w_out, a_w_in, a_g_q, a_g_k, a_w_out):
    B, S, D = x_prompt.shape
    DB, DS, _ = x_sample.shape
    assert DS == 1
    depth = ln_g.shape[0]
    H = m_b_i.shape[1]
    INNER = m_w_out.shape[1]
    QK = (m_w_in.shape[2] - 3 * INNER - 2 * H) // 2
    DQK, DV = QK // H, INNER // H
    ng, dh = a_g_q.shape[1], a_g_q.shape[2]
    GW = a_w_out.shape[1]
    nh = GW // dh
    assert ng == len(A_GROUPS) and dh == LANES
    caches = (cache_k1, cache_v1, cache_k2, cache_v2, cache_k3, cache_v3)

    xp = x_prompt.reshape(B * S, D)
    xs = x_sample.reshape(DB, D)
    tm_p = _row_tile(S, 1024)
    tm_s = DB
    cos_p, sin_p = _rope_tables(jnp.arange(S), dh)
    cos_s, sin_s = _rope_tables(jnp.full((DB,), PAST_LEN), dh)
    ones = jnp.ones((1, (ng + 1) * GW), F32)
    dils = tuple(dil for _, dil in A_GROUPS)

    mp, ms, ap, as_ = [], [], [], []
    c_s = None
    for i in range(depth):
        j = i // 2
        g = ln_g[i].reshape(1, D)
        if i % 2 == 0:
            assert (2 * QK + 3 * INNER) % 1024 == 0 and 2 * H < 1024
            w_main = m_w_in[j].astype(BF16)
            w_gate = jnp.pad(m_w_in[j][:, 2 * QK + 3 * INNER:], ((0, 0), (0, LANES - 2 * H))).astype(BF16)
            w_out = m_w_out[j].astype(BF16)
            bias = jnp.pad(jnp.concatenate([m_b_i[j], m_b_f[j]]), (0, LANES - 2 * H)).reshape(1, LANES)
            gh = m_g_h[j].reshape(1, INNER)

            p, gates = _proj_mlstm(xp, g, w_main, w_gate, tm=tm_p, tn=1024, out_dtype=BF16)
            xp, c_p, n_p, m_p = _mlstm_prompt(p, gates, bias, gh, w_out, xp, B=B, S=S, H=H, DQK=DQK, DV=DV)
            mp.append((c_p, n_p.reshape(B, H, DQK), m_p[:, 0, :H]))

            p, gates = _proj_mlstm(xs, g, w_main, w_gate, tm=tm_s, tn=1024, out_dtype=F32)
            m_in = jnp.pad(state_m[j], ((0, 0), (0, LANES - H)))
            y, c_s, n_s, m_s = _mlstm_sample(p, gates, bias, gh, state_C, j, c_s, state_n[j].reshape(DB, QK), m_in,
                                             H=H, DQK=DQK, DV=DV, bt=8)
            xs = _outproj(y, w_out, xs, tm=tm_s)
            ms.append((n_s.reshape(DB, H, DQK), m_s[:, :H]))
        else:
            w_in = a_w_in[j].astype(BF16)
            w_out = a_w_out[j].astype(BF16)
            gain = jnp.concatenate([jnp.tile(a_g_q[j], (1, nh)).reshape(1, ng * GW),
                                    jnp.tile(a_g_k[j], (1, nh)).reshape(1, ng * GW), ones], axis=1)

            qkvs, z = _proj_attn(xp, g, w_in, gain, cos_p, sin_p, B=B, tm=tm_p, tn=GW, ng=ng, dh=dh, dils=dils,
                                 out_dtype=BF16)
            os_, ls_ = zip(*[_attn_prompt(qkvs[gi], nh=nh, dh=dh, win=win) for gi, (win, _) in enumerate(A_GROUPS)])
            xp = _outproj_comb(os_, ls_, z, w_out, xp, B=B, tm=_row_tile(S, 512), nh=nh, dh=dh, dils=dils)
            rows = []
            for gi, (win, dil) in enumerate(A_GROUPS):
                n_keep = min(win, S) // dil
                tail = qkvs[gi][:, :, S // dil - n_keep:, GW:].astype(F32)
                tail = tail.transpose(0, 2, 1, 3).reshape(B, n_keep * dil, 2, nh, dh)
                rows += [tail[:, :, 0], tail[:, :, 1]]
            ap.append(rows)

            qkvs, z = _proj_attn(xs, g, w_in, gain, cos_s, sin_s, B=1, tm=tm_s, tn=GW, ng=ng, dh=dh, dils=(1,) * ng,
                                 out_dtype=F32)
            qkvs = [t.reshape(DB, 3, nh, dh) for t in qkvs]
            y = _attn_sample(qkvs, z.reshape(DB, nh, dh), caches, j, nh=nh, dh=dh, bt=4)
            xs = _outproj(y.reshape(DB, GW), w_out, xs, tm=tm_s)
            as_.append([t[:, c].reshape(DB, 1, nh, dh) for t in qkvs for c in (1, 2)])

    stack = lambda items, k: jnp.stack([it[k] for it in items])
    return ((xp.reshape(B, S, D), xs.reshape(DB, DS, D), stack(mp, 0), stack(mp, 1), stack(mp, 2))
            + tuple(stack(ap, k) for k in range(2 * ng))
            + (c_s, stack(ms, 0), stack(ms, 1))
            + tuple(stack(as_, k) for k in range(2 * ng)))
```

```python
import functools

import jax
import jax.numpy as jnp
from jax import lax
from jax.experimental import pallas as pl
from jax.experimental.pallas import tpu as pltpu

F32 = jnp.float32
BF16 = jnp.bfloat16
EPS = 1e-6
LANES = 128
VMEM_LIMIT = 56 * 1024 * 1024

A_GROUPS = ((128, 1), (512, 4), (2048, 16))
PAST_LEN = 2048
ROPE_THETA = 10000.0
A_BLK = 128
A_STEP_BLOCKS = 4
M_BLOCK = 256
PROJ_UNIT_ROWS = 512
GATHER_STRIDE = 4


def _params(n_axes):
    return pltpu.CompilerParams(dimension_semantics=("arbitrary",) * n_axes, vmem_limit_bytes=VMEM_LIMIT)


def _sigmoid(x):
    return 0.5 * jnp.tanh(0.5 * x) + 0.5


def _log_sigmoid(x):
    return jnp.minimum(x, 0.0) - jnp.log1p(jnp.exp(-jnp.abs(x)))


def _rms_rows(x, g):
    ms = jnp.mean(x * x, axis=-1, keepdims=True)
    return x * lax.rsqrt(ms + EPS) * g


def _split3(x):
    x1 = x.astype(BF16)
    r1 = x - x1.astype(F32)
    x2 = r1.astype(BF16)
    x3 = (r1 - x2.astype(F32)).astype(BF16)
    return x1, x2, x3


def _dot(a, b):
    return jnp.dot(a, b, preferred_element_type=F32)


def _dot_nt(a, b):
    return lax.dot_general(a, b, (((1,), (1,)), ((), ())), preferred_element_type=F32)


def _dot_tn(a, b):
    return lax.dot_general(a, b, (((0,), (0,)), ((), ())), preferred_element_type=F32)


def _proj_mlstm_kernel(x_ref, g_ref, w_ref, wg_ref, o_ref, gate_ref, xn_ref):
    @pl.when(pl.program_id(1) == 0)
    def _():
        xn = _rms_rows(x_ref[...], g_ref[...]).astype(BF16)
        xn_ref[...] = xn
        gate_ref[...] = _dot(xn, wg_ref[...])

    o_ref[...] = _dot(xn_ref[...], w_ref[...]).astype(o_ref.dtype)


def _proj_mlstm(x, g, w, wg, *, tm, tn, out_dtype):
    T, D = x.shape
    P = w.shape[1] // tn * tn
    return pl.pallas_call(
        _proj_mlstm_kernel,
        out_shape=(jax.ShapeDtypeStruct((T, P), out_dtype), jax.ShapeDtypeStruct((T, LANES), F32)),
        grid=(T // tm, P // tn),
        in_specs=[
            pl.BlockSpec((tm, D), lambda i, j: (i, 0)),
            pl.BlockSpec((1, D), lambda i, j: (0, 0)),
            pl.BlockSpec((D, tn), lambda i, j: (0, j)),
            pl.BlockSpec((D, LANES), lambda i, j: (0, 0)),
        ],
        out_specs=(
            pl.BlockSpec((tm, tn), lambda i, j: (i, j)),
            pl.BlockSpec((tm, LANES), lambda i, j: (i, 0)),
        ),
        scratch_shapes=[pltpu.VMEM((tm, D), BF16)],
        compiler_params=_params(2),
        name="proj_mlstm",
    )(x, g, w, wg)


def _proj_attn_kernel(x_ref, g_ref, w_ref, gain_ref, *refs, ng, dh, dils):
    cos_refs, sin_refs, refs = refs[:ng], refs[ng:2 * ng], refs[2 * ng:]
    outs, z_ref, xn_ref, acc_ref, acc2_ref = refs[:ng], refs[ng], refs[ng + 1], refs[ng + 2], refs[ng + 3]
    j = pl.program_id(1)
    n_slabs, tm, _ = acc_ref.shape

    @pl.when(j == 0)
    def _():
        xn_ref[...] = _rms_rows(x_ref[...], g_ref[...]).astype(BF16)

    @pl.when(j == 3 * ng)
    def _():
        z_ref[...] = _dot(xn_ref[...], w_ref[...]).astype(z_ref.dtype)

    def project(p):
        acc = _dot(xn_ref[...], w_ref[:, 2 * p * dh:2 * (p + 1) * dh])
        acc_ref[2 * p] = acc[:, :dh]
        acc_ref[2 * p + 1] = acc[:, dh:]

    def tile(g, dil, rot):
        per_res = tm // dil
        unit_rows = min(tm, PROJ_UNIT_ROWS)
        units = []
        for u in range(tm // unit_rows):
            lo = u * unit_rows
            if per_res >= unit_rows:
                units.append([(lo // per_res, lo % per_res, unit_rows)])
            else:
                units.append([(lo // per_res + k, 0, per_res) for k in range(unit_rows // per_res)])

        two_step = dil > GATHER_STRIDE and dil % GATHER_STRIDE == 0
        src_ref = acc2_ref if two_step else acc_ref

        def regroup(p):
            part = tm // GATHER_STRIDE
            for s in (2 * p, 2 * p + 1):
                for r1 in range(GATHER_STRIDE):
                    acc2_ref[s, r1 * part:(r1 + 1) * part, :] = acc_ref[s, pl.ds(r1, part, stride=GATHER_STRIDE), :]

        def gather(ref, seg, *lead):
            r, i0, n = seg
            if dil == 1:
                rows = pl.ds(i0, n)
            elif two_step:
                step = dil // GATHER_STRIDE
                rows = pl.ds((r % GATHER_STRIDE) * (tm // GATHER_STRIDE) + r // GATHER_STRIDE + step * i0, n, stride=step)
            else:
                rows = pl.ds(r + dil * i0, n, stride=dil)
            return ref[(*lead, rows, slice(None))]

        if rot:
            kk = lax.broadcasted_iota(jnp.int32, (2 * dh, 2 * dh), 0)
            cc = lax.broadcasted_iota(jnp.int32, (2 * dh, 2 * dh), 1)
            head_sum = jnp.where(_head_of(kk, dh) == _head_of(cc, dh), 1.0, 0.0).astype(BF16)
            half_swap = jnp.where(kk == (cc ^ (dh // 2)), 1.0, 0.0).astype(BF16)
        def finish(p):
            vals = [jnp.concatenate(
                [jnp.concatenate([gather(src_ref, sg, 2 * p + t) for sg in segs], axis=0) for t in range(2)], axis=1)
                for segs in units]
            if rot:
                tabs = []
                for u in range(len(units)):
                    cos = cos_refs[g][u * unit_rows:(u + 1) * unit_rows, :]
                    sin = sin_refs[g][u * unit_rows:(u + 1) * unit_rows, :]
                    tabs.append((jnp.concatenate([cos, cos], axis=1), jnp.concatenate([sin, sin], axis=1)))
                sums = [_dot((a * a).astype(BF16), head_sum) for a in vals]
                gain = gain_ref[:, 2 * p * dh:2 * (p + 1) * dh]
                vals = [a * lax.rsqrt(ss * (1.0 / dh) + EPS) * gain for a, ss in zip(vals, sums)]
                his = [a.astype(BF16) for a in vals]
                los = [(a - hi.astype(F32)).astype(BF16) for a, hi in zip(vals, his)]
                swapped = [_dot(hi, half_swap) + _dot(lo, half_swap) for hi, lo in zip(his, los)]
                vals = [a * cos + sw * sin for a, sw, (cos, sin) in zip(vals, swapped, tabs)]
            for segs, a in zip(units, vals):
                a = a.astype(outs[g].dtype)
                row = 0
                for r, i0, n in segs:
                    outs[g][r, i0:i0 + n, 2 * p * dh:2 * (p + 1) * dh] = a[row:row + n]
                    row += n

        n_pairs = n_slabs // 2
        project(0)
        for p in range(n_pairs):
            if p + 1 < n_pairs:
                project(p + 1)
            if two_step:
                regroup(p)
            finish(p)

    for g, dil in enumerate(dils):
        @pl.when((j < 2 * ng) & (lax.rem(j, ng) == g))
        def _(g=g, dil=dil):
            tile(g, dil, True)

        @pl.when((j >= 2 * ng) & (j < 3 * ng) & (lax.rem(j, ng) == g))
        def _(g=g, dil=dil):
            tile(g, dil, False)


def _proj_attn(x, g, w, gain, cos, sin, *, B, tm, tn, ng, dh, dils, out_dtype):
    T, D = x.shape
    S = T // B
    tiles = S // tm
    pos_blocks = cos.shape[0] // tm
    assert w.shape[1] == (3 * ng + 1) * tn and all(tm % d == 0 for d in dils)
    kern = functools.partial(_proj_attn_kernel, ng=ng, dh=dh, dils=dils)

    def tile_major(tab, dil):
        return tab.reshape(pos_blocks, tm // dil, dil, dh).transpose(0, 2, 1, 3).reshape(pos_blocks * tm, dh)

    tables = [tile_major(cos, dil) for dil in dils] + [tile_major(sin, dil) for dil in dils]
    table_spec = pl.BlockSpec((tm, dh), lambda i, j: (i % pos_blocks, 0))

    def group_spec(gi, dil):
        return pl.BlockSpec((None, dil, tm // dil, tn),
                            lambda i, j: (i // tiles, 0, i % tiles, jnp.clip(jnp.maximum(j - gi, 0) // ng, 0, 2)))

    *qkvs, z = pl.pallas_call(
        kern,
        out_shape=tuple(jax.ShapeDtypeStruct((B, dil, S // dil, 3 * tn), out_dtype) for dil in dils)
        + (jax.ShapeDtypeStruct((T, tn), out_dtype),),
        grid=(T // tm, 3 * ng + 1),
        in_specs=[
            pl.BlockSpec((tm, D), lambda i, j: (i, 0)),
            pl.BlockSpec((1, D), lambda i, j: (0, 0)),
            pl.BlockSpec((D, tn), lambda i, j: (0, j)),
            pl.BlockSpec((1, tn), lambda i, j: (0, j)),
        ] + [table_spec] * (2 * ng),
        out_specs=tuple(group_spec(gi, dil) for gi, dil in enumerate(dils))
        + (pl.BlockSpec((tm, tn), lambda i, j: (i, 0)),),
        scratch_shapes=[pltpu.VMEM((tm, D), BF16), pltpu.VMEM((tn // dh, tm, dh), F32),
                        pltpu.VMEM((tn // dh, tm, dh), F32)],
        compiler_params=_params(2),
        name="proj_attn",
    )(x, g, w, gain, *tables)
    return qkvs, z


def _outproj_kernel(y_ref, w_ref, r_ref, o_ref):
    o_ref[...] = r_ref[...] + _dot(y_ref[...].astype(BF16), w_ref[...])


def _outproj(y, w, resid, *, tm):
    T, E = y.shape
    D = w.shape[1]
    return pl.pallas_call(
        _outproj_kernel,
        out_shape=jax.ShapeDtypeStruct((T, D), F32),
        grid=(T // tm,),
        in_specs=[
            pl.BlockSpec((tm, E), lambda i: (i, 0)),
            pl.BlockSpec((E, D), lambda i: (0, 0)),
            pl.BlockSpec((tm, D), lambda i: (i, 0)),
        ],
        out_specs=pl.BlockSpec((tm, D), lambda i: (i, 0)),
        compiler_params=_params(1),
        name="outproj",
    )(y, w, resid)


def _head_of(col, dh):
    return lax.shift_right_logical(col, dh.bit_length() - 1)


def _head_expander(nh, dh):
    r = lax.broadcasted_iota(jnp.int32, (LANES, nh * dh), 0)
    c = lax.broadcasted_iota(jnp.int32, (LANES, nh * dh), 1)
    return jnp.where(_head_of(c, dh) == r, 1.0, 0.0).astype(BF16)


def _outproj_comb_kernel(*refs, ng, nh, dh, dils):
    o_refs, l_refs = refs[:ng], refs[ng:2 * ng]
    z_ref, w_ref, r_ref, out_ref, ot_ref, lt_ref = refs[2 * ng:]
    tm = out_ref.shape[0]
    for g, dil in enumerate(dils):
        for r in range(dil):
            rows = slice(None) if dil == 1 else pl.ds(r, tm // dil, stride=dil)
            lt_ref[g, rows, :] = l_refs[g][r]
            for s in range(nh):
                ot_ref[g, s, rows, :] = o_refs[g][r, :, s * dh:(s + 1) * dh].astype(F32)
    lses = [lt_ref[g] for g in range(ng)]
    mx = functools.reduce(jnp.maximum, lses)
    es = [jnp.exp(l - mx) for l in lses]
    inv = 1.0 / functools.reduce(jnp.add, es)
    expand = _head_expander(nh, dh)
    y = None
    for g in range(ng):
        w_hi, w_lo, _ = _split3(es[g] * inv)
        wide = _dot(w_hi, expand) + _dot(w_lo, expand)
        term = wide * jnp.concatenate([ot_ref[g, s] for s in range(nh)], axis=1)
        y = term if y is None else y + term
    z = z_ref[...].astype(F32)
    y = y * (z * _sigmoid(z))
    out_ref[...] = r_ref[...] + _dot(y.astype(BF16), w_ref[...])


def _outproj_comb(os_, ls_, z, w, resid, *, B, tm, nh, dh, dils):
    T, D = resid.shape
    E = w.shape[0]
    ng = len(dils)
    tiles = T // B // tm
    kern = functools.partial(_outproj_comb_kernel, ng=ng, nh=nh, dh=dh, dils=dils)
    res_spec = lambda dil, width: pl.BlockSpec((None, dil, tm // dil, width), lambda i: (i // tiles, 0, i % tiles, 0))
    return pl.pallas_call(
        kern,
        out_shape=jax.ShapeDtypeStruct((T, D), F32),
        grid=(T // tm,),
        in_specs=[res_spec(dil, E) for dil in dils] + [res_spec(dil, LANES) for dil in dils]
        + [pl.BlockSpec((tm, E), lambda i: (i, 0)),
           pl.BlockSpec((E, D), lambda i: (0, 0)),
           pl.BlockSpec((tm, D), lambda i: (i, 0))],
        out_specs=pl.BlockSpec((tm, D), lambda i: (i, 0)),
        scratch_shapes=[pltpu.VMEM((ng, nh, tm, dh), F32), pltpu.VMEM((ng, tm, LANES), F32)],
        compiler_params=_params(1),
        name="outproj_comb",
    )(*os_, *ls_, z, w, resid)


def _mlstm_prompt_kernel(q_ref, k_ref, v_ref, o_ref, z_ref, gt_ref, bias_ref, gh_ref, wo_ref, x_ref,
                         xo_ref, c_ref, n_ref, m_ref, *, H, DQK, DV):
    L = q_ref.shape[0]
    scale = DQK ** -0.5

    @pl.when(pl.program_id(1) == 0)
    def _():
        c_ref[...] = jnp.zeros_like(c_ref)
        n_ref[...] = jnp.zeros_like(n_ref)
        m_ref[...] = jnp.zeros_like(m_ref)

    gates = gt_ref[...] + bias_ref[...]
    lane = lax.broadcasted_iota(jnp.int32, gates.shape, 1)
    x = jnp.where(lane < H, gates, _log_sigmoid(gates))
    row = lax.broadcasted_iota(jnp.int32, (L, L), 0)
    col = lax.broadcasted_iota(jnp.int32, (L, L), 1)
    causal = col <= row
    tri = jnp.where(causal, 1.0, 0.0).astype(BF16)
    x1, x2, x3 = _split3(x)
    cum = _dot(tri, x1) + _dot(tri, x2) + _dot(tri, x3)
    xt = x.T
    cumt = cum.T

    m_all = m_ref[0]
    lane1 = lax.broadcasted_iota(jnp.int32, m_all.shape, 1)
    qss = [slice(h * DQK, (h + 1) * DQK) for h in range(H)]
    vss = [slice(h * DV, (h + 1) * DV) for h in range(H)]
    st = [dict() for _ in range(H)]
    gh4 = 0.25 * gh_ref[...]

    def stage_a(h):
        st[h]["qk"] = _dot_nt(q_ref[:, qss[h]], k_ref[:, qss[h]])
        st[h]["qc"] = _dot(q_ref[:, qss[h]], c_ref[0, h].astype(BF16))

    def stage_b(h):
        qs = qss[h]
        ig_row = xt[h:h + 1, :]
        ig_col = x[:, h:h + 1]
        b_row = cumt[H + h:H + h + 1, :]
        b_col = cum[:, H + h:H + h + 1]
        g_tot = cumt[H + h:H + h + 1, L - 1:L]
        m_prev = m_all[:, h:h + 1]

        dlog = jnp.where(causal, b_col - b_row + ig_row, -jnp.inf)
        inter = b_col + m_prev
        m_t = jnp.maximum(inter, jnp.max(dlog, axis=1, keepdims=True))
        w_inter = jnp.exp(inter - m_t)
        w_intra = jnp.exp(dlog - m_t) * (st[h].pop("qk") * scale)
        n_prev = n_ref[0, :, qs]
        den = (w_inter * jnp.sum(q_ref[:, qs].astype(F32) * n_prev, axis=1, keepdims=True)
               + jnp.sum(w_intra, axis=1, keepdims=True))
        inv = 1.0 / jnp.maximum(jnp.abs(den), jnp.exp(-m_t))

        a_col = ig_col + g_tot - b_col
        m_new = jnp.maximum(g_tot + m_prev, jnp.max(a_col, axis=0, keepdims=True))
        decay = jnp.exp(g_tot + m_prev - m_new)
        kw = k_ref[:, qs].astype(F32) * (jnp.exp(a_col - m_new) * scale)
        n_ref[0, :, qs] = decay * n_prev + jnp.sum(kw, axis=0, keepdims=True)
        st[h].update(w_inter=w_inter, w_intra=w_intra.astype(BF16), inv=inv, decay=decay, kw=kw.astype(BF16),
                     m_new=m_new)

    def stage_c(h):
        st[h]["wv"] = _dot(st[h].pop("w_intra"), v_ref[:, vss[h]])
        st[h]["kv"] = _dot_tn(st[h].pop("kw"), v_ref[:, vss[h]])

    def stage_d(h):
        vs = vss[h]
        hid = (st[h]["w_inter"] * st[h]["qc"] + st[h]["wv"]) * st[h]["inv"]
        hid = hid * lax.rsqrt(jnp.mean(hid * hid, axis=1, keepdims=True) + EPS)
        og = o_ref[:, vs].astype(F32)
        zg = z_ref[:, vs].astype(F32)
        gate = (jnp.tanh(0.5 * og) + 1.0) * (jnp.tanh(0.5 * zg) + 1.0) * zg
        y = (hid * gh4[:, vs] * gate).astype(BF16)
        st[0]["out"] = st[0].get("out", x_ref[...]) + _dot(y, wo_ref[vs, :])
        c_ref[0, h] = st[h]["decay"] * c_ref[0, h] + st[h]["kv"]

    order = [("a", 0)]
    for h in range(H):
        if h + 1 < H:
            order.append(("a", h + 1))
        order += [("b", h), ("c", h)]
        if h > 0:
            order.append(("d", h - 1))
    order.append(("d", H - 1))
    stages = dict(a=stage_a, b=stage_b, c=stage_c, d=stage_d)
    for name, h in order:
        stages[name](h)
    xo_ref[...] = st[0]["out"]
    m_next = m_all
    for h in range(H):
        m_next = jnp.where(lane1 == h, st[h]["m_new"], m_next)
    m_ref[0] = m_next


def _mlstm_prompt(p, gates, bias, gh, w_out, resid, *, B, S, H, DQK, DV):
    L = min(M_BLOCK, S)
    nc = S // L
    QK, INNER = H * DQK, H * DV
    D = w_out.shape[1]
    kern = functools.partial(_mlstm_prompt_kernel, H=H, DQK=DQK, DV=DV)
    row = lambda b, c: b * nc + c
    v_blk = 2 * QK // INNER
    return pl.pallas_call(
        kern,
        out_shape=(jax.ShapeDtypeStruct((B * S, D), F32),
                   jax.ShapeDtypeStruct((B, H, DQK, DV), F32),
                   jax.ShapeDtypeStruct((B, 1, QK), F32),
                   jax.ShapeDtypeStruct((B, 1, LANES), F32)),
        grid=(B, nc),
        in_specs=[
            pl.BlockSpec((L, QK), lambda b, c: (row(b, c), 0)),
            pl.BlockSpec((L, QK), lambda b, c: (row(b, c), 1)),
            pl.BlockSpec((L, INNER), lambda b, c: (row(b, c), v_blk)),
            pl.BlockSpec((L, INNER), lambda b, c: (row(b, c), v_blk + 1)),
            pl.BlockSpec((L, INNER), lambda b, c: (row(b, c), v_blk + 2)),
            pl.BlockSpec((L, LANES), lambda b, c: (row(b, c), 0)),
            pl.BlockSpec((1, LANES), lambda b, c: (0, 0)),
            pl.BlockSpec((1, INNER), lambda b, c: (0, 0)),
            pl.BlockSpec((INNER, D), lambda b, c: (0, 0)),
            pl.BlockSpec((L, D), lambda b, c: (row(b, c), 0)),
        ],
        out_specs=(
            pl.BlockSpec((L, D), lambda b, c: (row(b, c), 0)),
            pl.BlockSpec((1, H, DQK, DV), lambda b, c: (b, 0, 0, 0)),
            pl.BlockSpec((1, 1, QK), lambda b, c: (b, 0, 0)),
            pl.BlockSpec((1, 1, LANES), lambda b, c: (b, 0, 0)),
        ),
        compiler_params=_params(2),
        name="mlstm_prompt",
    )(p, p, p, p, p, gates, bias, gh, w_out, resid)


SC_ROWS = 16


def _mlstm_sample_kernel(p_ref, gt_ref, bias_ref, gh_ref, n_ref, m_ref, v_ref, c_ref, *refs, H, DQK, DV, bt, layer,
                         first):
    (y_ref, c_out_ref, n_out_ref, m_out_ref, qkt_ref, sct_ref, hq_ref, a_ref, bc_ref) = refs[-9:]
    if first:
        for other in range(c_out_ref.shape[0]):
            if other != layer:
                c_out_ref[other] = jnp.zeros(c_out_ref.shape[1:], F32)
        c_out_ref = c_out_ref.at[layer]
    step = pl.program_id(0)
    head = pl.program_id(1)
    DB = p_ref.shape[0]
    QK, INNER = H * DQK, H * DV
    scale = DQK ** -0.5

    @pl.when((step == 0) & (head == 0))
    def _():
        gates = gt_ref[...] + bias_ref[...]
        lane = lax.broadcasted_iota(jnp.int32, gates.shape, 1)
        ig = jnp.where(lane < H, gates, 0.0)
        lf = jnp.where(lane < H, pltpu.roll(_log_sigmoid(gates), LANES - H, axis=1), 0.0)
        m_prev = m_ref[...]
        m_t = jnp.maximum(lf + m_prev, ig)
        w_inter = jnp.exp(lf + m_prev - m_t)
        wa = jnp.exp(ig - m_t)
        q = p_ref[:, 0:QK].astype(F32)
        k = p_ref[:, QK:2 * QK].astype(F32)
        n_prev = n_ref[...]
        qk = jnp.zeros_like(gates)
        qn = jnp.zeros_like(gates)
        for h in range(H):
            qs = slice(h * DQK, (h + 1) * DQK)
            qk = jnp.where(lane == h, jnp.sum(q[:, qs] * k[:, qs], axis=1, keepdims=True) * scale, qk)
            qn = jnp.where(lane == h, jnp.sum(q[:, qs] * n_prev[:, qs], axis=1, keepdims=True), qn)
            n_out_ref[:, qs] = w_inter[:, h:h + 1] * n_prev[:, qs] + (wa[:, h:h + 1] * scale) * k[:, qs]
        w_intra = wa * qk
        den = w_inter * qn + w_intra
        inv = 1.0 / jnp.maximum(jnp.abs(den), jnp.exp(-m_t))
        a_ref[...] = w_inter * inv
        bc_ref[...] = w_intra * inv
        m_out_ref[...] = jnp.where(lane < H, m_t, 0.0)
        qkt_ref[...] = p_ref[:, 0:2 * QK].astype(F32).T.astype(BF16)
        sc = jnp.where(lane < H, w_inter, pltpu.roll(wa * scale, H, axis=1))
        sct_ref[...] = jnp.where(lane < 2 * H, sc, 0.0).T[0:SC_ROWS, :]

    r = lax.broadcasted_iota(jnp.int32, (DB, LANES), 0)
    srow = lax.broadcasted_iota(jnp.int32, (SC_ROWS, LANES), 0)
    trow = lax.broadcasted_iota(jnp.int32, (bt, DV), 0)
    qt = qkt_ref[pl.ds(pl.multiple_of(head * DQK, DQK), DQK), :]
    kt = qkt_ref[pl.ds(pl.multiple_of(QK + head * DQK, DQK), DQK), :]
    s1, s2, s3 = _split3(sct_ref[...])
    tile = jnp.zeros((bt, DV), F32)
    for i in range(bt):
        onehot = jnp.where(r == step * bt + i, 1.0, 0.0).astype(BF16)
        qcol = _dot(qt, onehot)
        kcol = _dot(kt, onehot)
        scal = _dot(s1, onehot) + _dot(s2, onehot) + _dot(s3, onehot)
        decay = jnp.sum(jnp.where(srow == head, scal, 0.0), axis=0, keepdims=True)
        wsc = jnp.sum(jnp.where(srow == H + head, scal, 0.0), axis=0, keepdims=True)
        hrow = []
        for t in range(DV // LANES):
            cs = slice(t * LANES, (t + 1) * LANES)
            c_prev = c_ref[i, 0, :, cs]
            hrow.append(jnp.sum(qcol * c_prev, axis=0, keepdims=True))
            c_out_ref[i, 0, :, cs] = decay * c_prev + kcol * (wsc * v_ref[i:i + 1, cs])
        tile = jnp.where(trow == i, jnp.concatenate(hrow, axis=1), tile)
    hq_ref[head, pl.ds(pl.multiple_of(step * bt, bt), bt), :] = tile

    @pl.when((step == pl.num_programs(0) - 1) & (head == H - 1))
    def _():
        for h in range(H):
            vs = slice(h * DV, (h + 1) * DV)
            v = p_ref[:, 2 * QK + h * DV:2 * QK + (h + 1) * DV].astype(F32)
            hid = a_ref[:, h:h + 1] * hq_ref[h] + bc_ref[:, h:h + 1] * v
            hid = hid * lax.rsqrt(jnp.mean(hid * hid, axis=1, keepdims=True) + EPS)
            og = p_ref[:, 2 * QK + INNER + h * DV:2 * QK + INNER + (h + 1) * DV].astype(F32)
            zg = p_ref[:, 2 * QK + 2 * INNER + h * DV:2 * QK + 2 * INNER + (h + 1) * DV].astype(F32)
            y_ref[:, vs] = (hid * gh_ref[:, vs] * _sigmoid(og) * (zg * _sigmoid(zg))).astype(BF16)


def _mlstm_sample(p, gates, bias, gh, c_all, layer, c_out_prev, n_in, m_in, *, H, DQK, DV, bt):
    DB = p.shape[0]
    QK, INNER = H * DQK, H * DV
    assert 2 * H <= SC_ROWS and DB % bt == 0
    first = c_out_prev is None
    kern = functools.partial(_mlstm_sample_kernel, H=H, DQK=DQK, DV=DV, bt=bt, layer=layer, first=first)
    full = lambda shape: pl.BlockSpec(shape, lambda s, h: (0,) * len(shape))
    c_spec = pl.BlockSpec((None, bt, 1, DQK, DV), lambda s, h: (layer, s, h, 0, 0))
    c_out_spec = pl.BlockSpec((c_all.shape[0], bt, 1, DQK, DV), lambda s, h: (0, s, h, 0, 0)) if first else c_spec
    v_heads = p[:, 2 * QK:2 * QK + INNER].reshape(DB, H, DV).transpose(1, 0, 2)
    args = [p, gates, bias, gh, n_in, m_in, v_heads, c_all]
    in_specs = [full(p.shape), full((DB, LANES)), full((1, LANES)), full((1, INNER)),
                full((DB, QK)), full((DB, LANES)),
                pl.BlockSpec((None, bt, DV), lambda s, h: (h, s, 0)), c_spec]
    aliases = {}
    if c_out_prev is not None:
        aliases = {len(args): 1}
        args.append(c_out_prev)
        in_specs.append(pl.BlockSpec(memory_space=pl.ANY))
    return pl.pallas_call(
        kern,
        out_shape=(jax.ShapeDtypeStruct((DB, INNER), BF16),
                   jax.ShapeDtypeStruct(c_all.shape, F32),
                   jax.ShapeDtypeStruct((DB, QK), F32),
                   jax.ShapeDtypeStruct((DB, LANES), F32)),
        grid=(DB // bt, H),
        in_specs=in_specs,
        out_specs=(full((DB, INNER)), c_out_spec, full((DB, QK)), full((DB, LANES))),
        scratch_shapes=[pltpu.VMEM((2 * QK, DB), BF16), pltpu.VMEM((SC_ROWS, DB), F32),
                        pltpu.VMEM((H, DB, DV), F32), pltpu.VMEM((DB, LANES), F32), pltpu.VMEM((DB, LANES), F32)],
        input_output_aliases=aliases,
        compiler_params=_params(2),
        name="mlstm_sample",
    )(*args)


def _attn_prompt_kernel(q_ref, k_ref, v_ref, o_ref, lse_ref, kp_ref, vp_ref, *, nh, dh, span, blk):
    n = pl.program_id(2)
    nq = q_ref.shape[0] // blk
    scale = dh ** -0.5

    @pl.when(n == 0)
    def _():
        kp_ref[...] = jnp.zeros_like(kp_ref)
        vp_ref[...] = jnp.zeros_like(vp_ref)

    iq = lax.broadcasted_iota(jnp.int32, (blk, blk), 0)
    ik = lax.broadcasted_iota(jnp.int32, (blk, blk), 1)
    mask_cur = (ik <= iq) & (iq - ik <= span)
    mask_prev = blk + iq - ik <= span
    mask_first = blk + iq - ik <= jnp.where(n > 0, span, -1)
    lane = lax.broadcasted_iota(jnp.int32, (blk, LANES), 1)
    heads = [slice(h * dh, (h + 1) * dh) for h in range(nh)]

    def rows(t):
        return slice(t * blk, (t + 1) * blk)

    def prev_block(ref, carry_ref, t, hs):
        return carry_ref[:, hs] if t == 0 else ref[rows(t - 1), hs]

    for t0 in range(0, nq, 2):
        units = [(t, h) for t in range(t0, min(t0 + 2, nq)) for h in range(nh)]
        scores = {(t, h): (_dot_nt(q_ref[rows(t), heads[h]], k_ref[rows(t), heads[h]]),
                           _dot_nt(q_ref[rows(t), heads[h]], prev_block(k_ref, kp_ref, t, heads[h])))
                  for t, h in units}
        probs = {}
        lse = {t: jnp.zeros((blk, LANES), F32) for t, _ in units}
        for t, h in units:
            s_cur, s_prev = scores[t, h]
            s_cur = jnp.where(mask_cur, s_cur * scale, -jnp.inf)
            s_prev = jnp.where(mask_first if t == 0 else mask_prev, s_prev * scale, -jnp.inf)
            mx = jnp.maximum(jnp.max(s_cur, axis=1, keepdims=True), jnp.max(s_prev, axis=1, keepdims=True))
            p_cur = jnp.exp(s_cur - mx)
            p_prev = jnp.exp(s_prev - mx)
            den = jnp.sum(p_cur, axis=1, keepdims=True) + jnp.sum(p_prev, axis=1, keepdims=True)
            inv = 1.0 / den
            probs[t, h] = ((p_cur * inv).astype(BF16), (p_prev * inv).astype(BF16))
            lse[t] = jnp.where(lane == h, mx + jnp.log(den), lse[t])
        for t, h in units:
            p_cur, p_prev = probs[t, h]
            o_ref[rows(t), heads[h]] = (_dot(p_cur, v_ref[rows(t), heads[h]])
                                        + _dot(p_prev, prev_block(v_ref, vp_ref, t, heads[h]))).astype(BF16)
        for t in lse:
            lse_ref[rows(t), :] = lse[t]
    kp_ref[...] = k_ref[rows(nq - 1), :]
    vp_ref[...] = v_ref[rows(nq - 1), :]


def _attn_prompt(qkv, *, nh, dh, win):
    B, dil, Ls, _ = qkv.shape
    GW = nh * dh
    rows = min(A_STEP_BLOCKS * A_BLK, Ls)
    assert Ls % rows == 0
    kern = functools.partial(_attn_prompt_kernel, nh=nh, dh=dh, span=win // dil, blk=A_BLK)
    spec = lambda width, col: pl.BlockSpec((None, None, rows, width), lambda b, r, n: (b, r, n, col))
    return pl.pallas_call(
        kern,
        out_shape=(jax.ShapeDtypeStruct((B, dil, Ls, GW), BF16),
                   jax.ShapeDtypeStruct((B, dil, Ls, LANES), F32)),
        grid=(B, dil, Ls // rows),
        in_specs=[spec(GW, 0), spec(GW, 1), spec(GW, 2)],
        out_specs=(spec(GW, 0), spec(LANES, 0)),
        scratch_shapes=[pltpu.VMEM((A_BLK, GW), BF16), pltpu.VMEM((A_BLK, GW), BF16)],
        compiler_params=_params(3),
        name=f"attn_prompt_d{dil}",
    )(qkv, qkv, qkv)


def _attn_sample_kernel(*refs, ng, nh, dh):
    qkv_refs, z_ref = refs[:ng], refs[ng]
    cache_refs, y_ref = refs[ng + 1:3 * ng + 1], refs[3 * ng + 1]
    bt = z_ref.shape[0]
    scale = dh ** -0.5
    pad = jnp.zeros((nh, dh), F32)
    rows16 = lambda t: jnp.concatenate([t, pad], axis=0)
    cases = [(i, g) for i in range(bt) for g in range(ng)]
    flat = lambda ref, i: ref[i].reshape(ref.shape[1] * nh, dh).astype(BF16)
    qs = {(i, g): rows16(qkv_refs[g][i, 0]) for i, g in cases}
    scores = {(i, g): _dot_nt(qs[i, g].astype(BF16), flat(cache_refs[2 * g], i)) for i, g in cases}
    probs = {}
    for i, g in cases:
        q, kn = qs[i, g], rows16(qkv_refs[g][i, 1])
        span = cache_refs[2 * g].shape[1]
        head = lax.broadcasted_iota(jnp.int32, (2 * nh, span * nh), 0)
        key_head = lax.broadcasted_iota(jnp.int32, (2 * nh, span * nh), 1) & (nh - 1)
        s = jnp.where(key_head == head, scores[i, g] * scale, -jnp.inf)
        s0 = jnp.sum(q * kn, axis=1, keepdims=True) * scale
        mx = jnp.maximum(jnp.max(s, axis=1, keepdims=True), s0)
        pr = jnp.exp(s - mx)
        p0 = jnp.exp(s0 - mx)
        den = jnp.sum(pr, axis=1, keepdims=True) + p0
        inv = 1.0 / den
        probs[i, g] = ((pr * inv).astype(BF16), p0 * inv, mx + jnp.log(den))
    outs = {(i, g): _dot(probs[i, g][0], flat(cache_refs[2 * g + 1], i)) for i, g in cases}
    for i in range(bt):
        lses = [probs[i, g][2] for g in range(ng)]
        mxl = functools.reduce(jnp.maximum, lses)
        es = [jnp.exp(l - mxl) for l in lses]
        inv = 1.0 / functools.reduce(jnp.add, es)
        terms = [(es[g] * inv) * (outs[i, g] + probs[i, g][1] * rows16(qkv_refs[g][i, 2])) for g in range(ng)]
        z = z_ref[i]
        y_ref[i] = functools.reduce(jnp.add, terms)[0:nh] * (z * _sigmoid(z))


def _attn_sample(qkvs, z, caches, layer, *, nh, dh, bt):
    DB = z.shape[0]
    ng = len(qkvs)
    assert nh & (nh - 1) == 0
    views, specs = [], []
    for g, (win, dil) in enumerate(A_GROUPS):
        span = win // dil
        for c in caches[2 * g:2 * g + 2]:
            assert c.shape[2] == win, "decode path expects a full window of cached rows"
            views.append(c.reshape(c.shape[0], DB, span, dil, nh, dh))
            specs.append(pl.BlockSpec((None, bt, span, None, nh, dh), lambda s: (layer, s, 0, 0, 0, 0)))
    kern = functools.partial(_attn_sample_kernel, ng=ng, nh=nh, dh=dh)
    return pl.pallas_call(
        kern,
        out_shape=jax.ShapeDtypeStruct((DB, nh, dh), F32),
        grid=(DB // bt,),
        in_specs=[pl.BlockSpec((bt, 3, nh, dh), lambda s: (s, 0, 0, 0))] * ng
        + [pl.BlockSpec((bt, nh, dh), lambda s: (s, 0, 0))] + specs,
        out_specs=pl.BlockSpec((bt, nh, dh), lambda s: (s, 0, 0)),
        compiler_params=_params(1),
        name="attn_sample",
    )(*qkvs, z, *views)


def _rope_tables(pos, dh):
    half = dh // 2
    inv = ROPE_THETA ** (-jnp.arange(half, dtype=F32) / half)
    ang = pos.astype(F32)[:, None] * inv[None, :]
    cos, sin = jnp.cos(ang), jnp.sin(ang)
    return jnp.concatenate([cos, cos], axis=1), jnp.concatenate([-sin, sin], axis=1)


def _row_tile(t, cap):
    tm = min(t, cap)
    while t % tm:
        tm //= 2
    return tm


def kernel(x_prompt, x_sample, state_C, state_n, state_m, cache_k1, cache_v1, cache_k2, cache_v2, cache_k3, cache_v3, ln_g, m_w_in, m_b_i, m_b_f, m_g_h, m_w_out, a_w_in, a_g_q, a_g_k, a_w_out):
    B, S, D = x_prompt.shape
    DB, DS, _ = x_sample.shape
    assert DS == 1
    depth = ln_g.shape[0]
    H = m_b_i.shape[1]
    INNER = m_w_out.shape[1]
    QK = (m_w_in.shape[2] - 3 * INNER - 2 * H) // 2
    DQK, DV = QK // H, INNER // H
    ng, dh = a_g_q.shape[1], a_g_q.shape[2]
    GW = a_w_out.shape[1]
    nh = GW // dh
    assert ng == len(A_GROUPS) and dh == LANES
    caches = (cache_k1, cache_v1, cache_k2, cache_v2, cache_k3, cache_v3)

    xp = x_prompt.reshape(B * S, D)
    xs = x_sample.reshape(DB, D)
    tm_p = _row_tile(S, 1024)
    tm_s = DB
    cos_p, sin_p = _rope_tables(jnp.arange(S), dh)
    cos_s, sin_s = _rope_tables(jnp.full((DB,), PAST_LEN), dh)
    ones = jnp.ones((1, (ng + 1) * GW), F32)
    dils = tuple(dil for _, dil in A_GROUPS)

    mp, ms, ap, as_ = [], [], [], []
    c_s = None
    for i in range(depth):
        j = i // 2
        g = ln_g[i].reshape(1, D)
        if i % 2 == 0:
            assert (2 * QK + 3 * INNER) % 1024 == 0 and 2 * H < 1024
            w_main = m_w_in[j].astype(BF16)
            w_gate = jnp.pad(m_w_in[j][:, 2 * QK + 3 * INNER:], ((0, 0), (0, LANES - 2 * H))).astype(BF16)
            w_out = m_w_out[j].astype(BF16)
            bias = jnp.pad(jnp.concatenate([m_b_i[j], m_b_f[j]]), (0, LANES - 2 * H)).reshape(1, LANES)
            gh = m_g_h[j].reshape(1, INNER)

            p, gates = _proj_mlstm(xp, g, w_main, w_gate, tm=tm_p, tn=1024, out_dtype=BF16)
            xp, c_p, n_p, m_p = _mlstm_prompt(p, gates, bias, gh, w_out, xp, B=B, S=S, H=H, DQK=DQK, DV=DV)
            mp.append((c_p, n_p.reshape(B, H, DQK), m_p[:, 0, :H]))

            p, gates = _proj_mlstm(xs, g, w_main, w_gate, tm=tm_s, tn=1024, out_dtype=F32)
            m_in = jnp.pad(state_m[j], ((0, 0), (0, LANES - H)))
            y, c_s, n_s, m_s = _mlstm_sample(p, gates, bias, gh, state_C, j, c_s, state_n[j].reshape(DB, QK), m_in,
                                             H=H, DQK=DQK, DV=DV, bt=8)
            xs = _outproj(y, w_out, xs, tm=tm_s)
            ms.append((n_s.reshape(DB, H, DQK), m_s[:, :H]))
        else:
            w_in = a_w_in[j].astype(BF16)
            w_out = a_w_out[j].astype(BF16)
            gain = jnp.concatenate([jnp.tile(a_g_q[j], (1, nh)).reshape(1, ng * GW),
                                    jnp.tile(a_g_k[j], (1, nh)).reshape(1, ng * GW), ones], axis=1)

            qkvs, z = _proj_attn(xp, g, w_in, gain, cos_p, sin_p, B=B, tm=tm_p, tn=GW, ng=ng, dh=dh, dils=dils,
                                 out_dtype=BF16)
            os_, ls_ = zip(*[_attn_prompt(qkvs[gi], nh=nh, dh=dh, win=win) for gi, (win, _) in enumerate(A_GROUPS)])
            xp = _outproj_comb(os_, ls_, z, w_out, xp, B=B, tm=_row_tile(S, 512), nh=nh, dh=dh, dils=dils)
            rows = []
            for gi, (win, dil) in enumerate(A_GROUPS):
                n_keep = min(win, S) // dil
                tail = qkvs[gi][:, :, S // dil - n_keep:, GW:].astype(F32)
                tail = tail.transpose(0, 2, 1, 3).reshape(B, n_keep * dil, 2, nh, dh)
                rows += [tail[:, :, 0], tail[:, :, 1]]
            ap.append(rows)

            qkvs, z = _proj_attn(xs, g, w_in, gain, cos_s, sin_s, B=1, tm=tm_s, tn=GW, ng=ng, dh=dh, dils=(1,) * ng,
                                 out_dtype=F32)
            qkvs = [t.reshape(DB, 3, nh, dh) for t in qkvs]
            y = _attn_sample(qkvs, z.reshape(DB, nh, dh), caches, j, nh=nh, dh=dh, bt=4)
            xs = _outproj(y.reshape(DB, GW), w_out, xs, tm=tm_s)
            as_.append([t[:, c].reshape(DB, 1, nh, dh) for t in qkvs for c in (1, 2)])

    stack = lambda items, k: jnp.stack([it[k] for it in items])
    return ((xp.reshape(B, S, D), xs.reshape(DB, DS, D), stack(mp, 0), stack(mp, 1), stack(mp, 2))
            + tuple(stack(ap, k) for k in range(2 * ng))
            + (c_s, stack(ms, 0), stack(ms, 1))
            + tuple(stack(as_, k) for k in range(2 * ng)))
```
